```python
import math
import jax
import jax.numpy as jnp
from jax import lax
import numpy as np

D_MODEL = 2048
BATCH = 2
SEQ = 16384
DEPTH = 2

CTX_LEN = 256
GRID_W = 64
ATT_HEADS = 8
ATT_HD = 64
ATT_VD = 2 * ATT_HD
ROPE_BASE = 10000.0
Q_BLOCK = 128
SGU_GROUPS = 8
SGU_CH = 128
SGU_CHUNK = 128
FOURIER_GROUPS = 4
FOURIER_CH = 256
MLP_HIDDEN = 4 * D_MODEL
N_MOD = 6
EPS = 1e-6
SUBLN_EPS = 1e-5

ATT_QK_WIDTH = ATT_HEADS * ATT_HD
ATT_V_WIDTH = ATT_HEADS * ATT_VD
SGU_WIDTH = SGU_GROUPS * SGU_CH
FOURIER_WIDTH = FOURIER_GROUPS * FOURIER_CH
SEGMENTS = (
    ("q1", ATT_QK_WIDTH), ("q2", ATT_QK_WIDTH), ("k1", ATT_QK_WIDTH), ("k2", ATT_QK_WIDTH),
    ("v", ATT_V_WIDTH), ("su", SGU_WIDTH), ("sv", SGU_WIDTH), ("f", FOURIER_WIDTH),
    ("ga", D_MODEL), ("gg", D_MODEL), ("gf", D_MODEL),
)
IN_WIDTH = 4 * ATT_QK_WIDTH + ATT_V_WIDTH + 2 * SGU_WIDTH + FOURIER_WIDTH + 3 * D_MODEL

kernel_name = "hybrid_diffattn_sgu_fourier_dit_block"


def seg_range(name):
    start = 0
    for n, w in SEGMENTS:
        if n == name:
            return start, start + w
        start += w
    raise KeyError(name)


def seg(p, name, base=0):
    a, b = seg_range(name)
    return p[..., a - base:b - base]


def rmsnorm(x, g, eps=EPS):
    xf = x.astype(jnp.float32)
    y = xf * lax.rsqrt(jnp.mean(xf * xf, axis=-1, keepdims=True) + eps)
    return (y * g.astype(jnp.float32)).astype(x.dtype)


def modulate(h, shift, scale):
    return h * (1.0 + scale) + shift


def heads(t, hd):
    return t.reshape(t.shape[0], t.shape[1], ATT_HEADS, hd)


def axial_rope_tables(n_tokens):
    n_rows = n_tokens // GRID_W
    row = jnp.broadcast_to(jnp.arange(n_rows, dtype=jnp.float32)[:, None], (n_rows, GRID_W)).reshape(-1)
    col = jnp.broadcast_to(jnp.arange(GRID_W, dtype=jnp.float32)[None, :], (n_rows, GRID_W)).reshape(-1)
    n_freq = ATT_HD // 4
    inv = ROPE_BASE ** (-jnp.arange(n_freq, dtype=jnp.float32) / n_freq)
    ar = row[:, None] * inv
    ac = col[:, None] * inv
    ang = jnp.concatenate([ar, ar, ac, ac], axis=-1)
    return jnp.cos(ang), jnp.sin(ang)


def rotate_half(t):
    t1, t2 = jnp.split(t, 2, axis=-1)
    return jnp.concatenate([-t2, t1], axis=-1)


def apply_axial_rope(x, cos, sin):
    xf = x.astype(jnp.float32)
    half = ATT_HD // 2
    rot = jnp.concatenate([rotate_half(xf[..., :half]), rotate_half(xf[..., half:])], axis=-1)
    return (xf * cos[None, :, None, :] + rot * sin[None, :, None, :]).astype(x.dtype)


def diff_attend(q1, q2, k1, k2, v, lam):
    scale = ATT_HD ** -0.5
    s1 = jnp.einsum("bqhd,bkhd->bhqk", q1, k1).astype(jnp.float32) * scale
    s2 = jnp.einsum("bqhd,bkhd->bhqk", q2, k2).astype(jnp.float32) * scale
    a = jax.nn.softmax(s1, axis=-1) - lam * jax.nn.softmax(s2, axis=-1)
    return jnp.einsum("bhqk,bkhe->bqhe", a.astype(v.dtype), v)


def blocked_diff_attend(q1, q2, k1, k2, v, lam):
    b, n, h, d = q1.shape
    nblk = n // Q_BLOCK

    def to_blocks(q):
        return q.reshape(b, nblk, Q_BLOCK, h, d).transpose(1, 0, 2, 3, 4)

    out = lax.map(lambda qq: diff_attend(qq[0], qq[1], k1, k2, v, lam), (to_blocks(q1), to_blocks(q2)))
    return out.transpose(1, 0, 2, 3, 4).reshape(b, n, h, ATT_VD)


def diff_head_out(o, g, lambda_init):
    o = rmsnorm(o, g, SUBLN_EPS) * (1.0 - lambda_init)
    return o.reshape(o.shape[0], o.shape[1], ATT_V_WIDTH)


def spatial_gating(u, v, w_s, b_s, g_v):
    bsz, L, _ = v.shape
    v = rmsnorm(v, g_v)
    nch = L // SGU_CHUNK
    v = v.reshape(bsz, nch, SGU_CHUNK, SGU_GROUPS, SGU_CH)
    mixed = jnp.einsum("gpq,bnqgc->bnpgc", w_s, v) + b_s.T[None, None, :, :, None]
    return u * mixed.reshape(bsz, L, SGU_WIDTH)


def fourier_mix(f):
    bsz, L, _ = f.shape
    fg = f.reshape(bsz, L, FOURIER_GROUPS, FOURIER_CH).astype(jnp.float32)
    y = jnp.fft.fft2(fg, axes=(1, 3), norm="ortho").real
    return y.astype(f.dtype).reshape(bsz, L, FOURIER_WIDTH)


def merge_branches(p, att_o, w_s, b_s, g_v, w_pa, w_ps, w_pf, w_o):
    sgu_o = spatial_gating(jax.nn.gelu(seg(p, "su"), approximate=False),
                           jax.nn.gelu(seg(p, "sv"), approximate=False), w_s, b_s, g_v)
    four_o = fourier_mix(seg(p, "f"))
    y = (jax.nn.sigmoid(seg(p, "ga")) * (att_o @ w_pa)
         + jax.nn.sigmoid(seg(p, "gg")) * (sgu_o @ w_ps)
         + jax.nn.sigmoid(seg(p, "gf")) * (four_o @ w_pf))
    return y @ w_o


def sq_relu_mlp(h, w1, w2):
    return jnp.square(jax.nn.relu(h @ w1)) @ w2


def setup_inputs(seed: int = 0) -> dict:
    key = jax.random.key(seed)
    ks = jax.random.split(key, 24)
    f32 = jnp.float32
    nrm = lambda k, shape, s: jax.random.normal(k, shape, f32) * s
    gain = lambda k, shape: 1.0 + 0.02 * jax.random.normal(k, shape, f32)
    return {
        "x": nrm(ks[0], (BATCH, SEQ, D_MODEL), 1.0),
        "c": nrm(ks[1], (BATCH, D_MODEL), 1.0),
        "ctx": nrm(ks[2], (BATCH, CTX_LEN, D_MODEL), 1.0),
        "c_ctx": nrm(ks[3], (D_MODEL,), 1.0),
        "w_mod": nrm(ks[4], (DEPTH, D_MODEL, N_MOD * D_MODEL), 0.5 * D_MODEL ** -0.5),
        "b_mod": nrm(ks[5], (DEPTH, N_MOD * D_MODEL), 0.02),
        "norm1_g": gain(ks[6], (DEPTH, D_MODEL)),
        "norm2_g": gain(ks[7], (DEPTH, D_MODEL)),
        "w_in": nrm(ks[8], (DEPTH, D_MODEL, IN_WIDTH), D_MODEL ** -0.5),
        "lambda_q1": nrm(ks[9], (DEPTH, ATT_HD), 0.1),
        "lambda_k1": nrm(ks[10], (DEPTH, ATT_HD), 0.1),
        "lambda_q2": nrm(ks[11], (DEPTH, ATT_HD), 0.1),
        "lambda_k2": nrm(ks[12], (DEPTH, ATT_HD), 0.1),
        "subln_g": gain(ks[13], (DEPTH, ATT_VD)),
        "sgu_norm_g": gain(ks[14], (DEPTH, SGU_WIDTH)),
        "sgu_w": nrm(ks[15], (DEPTH, SGU_GROUPS, SGU_CHUNK, SGU_CHUNK), SGU_CHUNK ** -0.5),
        "sgu_b": gain(ks[16], (DEPTH, SGU_GROUPS, SGU_CHUNK)),
        "w_proj_att": nrm(ks[17], (DEPTH, ATT_V_WIDTH, D_MODEL), ATT_V_WIDTH ** -0.5),
        "w_proj_sgu": nrm(ks[18], (DEPTH, SGU_WIDTH, D_MODEL), SGU_WIDTH ** -0.5),
        "w_proj_fourier": nrm(ks[19], (DEPTH, FOURIER_WIDTH, D_MODEL), FOURIER_WIDTH ** -0.5),
        "w_out": nrm(ks[20], (DEPTH, D_MODEL, D_MODEL), D_MODEL ** -0.5),
        "w_mlp_in": nrm(ks[21], (DEPTH, D_MODEL, MLP_HIDDEN), D_MODEL ** -0.5),
        "w_mlp_out": nrm(ks[22], (DEPTH, MLP_HIDDEN, D_MODEL), MLP_HIDDEN ** -0.5),
        "final_g": gain(ks[23], (D_MODEL,)),
    }


def reference(x, c, ctx, c_ctx, w_mod, b_mod, norm1_g, norm2_g, w_in, lambda_q1, lambda_k1, lambda_q2,
              lambda_k2, subln_g, sgu_norm_g, sgu_w, sgu_b, w_proj_att, w_proj_sgu, w_proj_fourier, w_out,
              w_mlp_in, w_mlp_out, final_g):
    n_tok = x.shape[1]
    cos, sin = axial_rope_tables(n_tok)
    for l in range(DEPTH):
        last = l == DEPTH - 1
        lambda_init = 0.8 - 0.6 * math.exp(-0.3 * l)
        lam = (jnp.exp(jnp.sum(lambda_q1[l].astype(jnp.float32) * lambda_k1[l].astype(jnp.float32)))
               - jnp.exp(jnp.sum(lambda_q2[l].astype(jnp.float32) * lambda_k2[l].astype(jnp.float32)))
               + lambda_init)
        mod = jax.nn.silu(c) @ w_mod[l] + b_mod[l]
        sh1, sc1, gt1, sh2, sc2, gt2 = [m[:, None, :] for m in jnp.split(mod, N_MOD, axis=-1)]
        mod_c = jax.nn.silu(c_ctx) @ w_mod[l] + b_mod[l]
        csh1, csc1, cgt1, csh2, csc2, cgt2 = jnp.split(mod_c, N_MOD, axis=-1)

        hx = modulate(rmsnorm(x, norm1_g[l]), sh1, sc1)
        hc = modulate(rmsnorm(ctx, norm1_g[l]), csh1, csc1)

        if last:
            base = seg_range("k1")[0]
            pc = hc @ w_in[l][:, base:seg_range("v")[1]]
        else:
            base = 0
            pc = hc @ w_in[l]
        kc1 = heads(seg(pc, "k1", base), ATT_HD)
        kc2 = heads(seg(pc, "k2", base), ATT_HD)
        vc = heads(seg(pc, "v", base), ATT_VD)

        px = hx @ w_in[l]
        q1 = apply_axial_rope(heads(seg(px, "q1"), ATT_HD), cos, sin)
        q2 = apply_axial_rope(heads(seg(px, "q2"), ATT_HD), cos, sin)
        k1 = jnp.concatenate([apply_axial_rope(heads(seg(px, "k1"), ATT_HD), cos, sin), kc1], axis=1)
        k2 = jnp.concatenate([apply_axial_rope(heads(seg(px, "k2"), ATT_HD), cos, sin), kc2], axis=1)
        v = jnp.concatenate([heads(seg(px, "v"), ATT_VD), vc], axis=1)
        att_x = diff_head_out(blocked_diff_attend(q1, q2, k1, k2, v, lam), subln_g[l], lambda_init)
        mix_x = merge_branches(px, att_x, sgu_w[l], sgu_b[l], sgu_norm_g[l], w_proj_att[l], w_proj_sgu[l],
                               w_proj_fourier[l], w_out[l])

        if not last:
            qc1 = heads(seg(pc, "q1"), ATT_HD)
            qc2 = heads(seg(pc, "q2"), ATT_HD)
            att_c = diff_head_out(diff_attend(qc1, qc2, kc1, kc2, vc, lam), subln_g[l], lambda_init)
            mix_c = merge_branches(pc, att_c, sgu_w[l], sgu_b[l], sgu_norm_g[l], w_proj_att[l], w_proj_sgu[l],
                                   w_proj_fourier[l], w_out[l])
            ctx = ctx + cgt1 * mix_c
            hc2 = modulate(rmsnorm(ctx, norm2_g[l]), csh2, csc2)
            ctx = ctx + cgt2 * sq_relu_mlp(hc2, w_mlp_in[l], w_mlp_out[l])

        x = x + gt1 * mix_x
        hx2 = modulate(rmsnorm(x, norm2_g[l]), sh2, sc2)
        x = x + gt2 * sq_relu_mlp(hx2, w_mlp_in[l], w_mlp_out[l])
    return rmsnorm(x, final_g)
```

```python
import functools
import math

import jax
import jax.numpy as jnp
from jax import lax
from jax.experimental import pallas as pl
from jax.experimental.pallas import tpu as pltpu

F32 = jnp.float32
BF16 = jnp.bfloat16

GRID_W = 64
ATT_HEADS = 8
ATT_HD = 64
ATT_VD = 2 * ATT_HD
ROPE_BASE = 10000.0
SGU_GROUPS = 8
SGU_CH = 128
SGU_CHUNK = 128
FOURIER_GROUPS = 4
FOURIER_CH = 256
N_MOD = 6
EPS = 1e-6
SUBLN_EPS = 1e-5

LANES = 128
BF16_SUBLANES = 16
F32_SUBLANES = 8
VMEM_LIMIT_BYTES = 56 * 2**20

LOG2E = 1.4426950408889634
Q_PRESCALE = (ATT_HD ** -0.5) * LOG2E
NEG_BIG = -1e30


def _cparams(n_axes):
    return pltpu.CompilerParams(dimension_semantics=("arbitrary",) * n_axes,
                                vmem_limit_bytes=VMEM_LIMIT_BYTES)


def _dot(a, b):
    return jnp.dot(a, b, preferred_element_type=F32)


def _resident(shape, index_map):
    return pl.BlockSpec(shape, index_map, pipeline_mode=pl.Buffered(1))


def _mod_kernel(c_ref, w_ref, b_ref, o_ref):
    a = c_ref[...]
    a = a * jax.nn.sigmoid(a)
    w = w_ref[...]
    a_hi = a.astype(BF16)
    a_lo = (a - a_hi.astype(F32)).astype(BF16)
    w_hi = w.astype(BF16)
    w_lo = (w - w_hi.astype(F32)).astype(BF16)
    o_ref[...] = _dot(a_hi, w_hi) + _dot(a_hi, w_lo) + _dot(a_lo, w_hi) + b_ref[...]


def _modulation(c_rows, w_mod, b_mod):
    depth, d, n = w_mod.shape
    tn = 512
    rows = c_rows.shape[0]
    return pl.pallas_call(
        _mod_kernel,
        grid=(depth, n // tn),
        in_specs=[
            pl.BlockSpec((rows, d), lambda l, j: (0, 0)),
            pl.BlockSpec((None, d, tn), lambda l, j: (l, 0, j)),
            pl.BlockSpec((None, 1, tn), lambda l, j: (l, 0, j)),
        ],
        out_specs=pl.BlockSpec((None, rows, tn), lambda l, j: (l, 0, j)),
        out_shape=jax.ShapeDtypeStruct((depth, rows, n), F32),
        compiler_params=_cparams(2),
        name="adaln_mod",
    )(c_rows, w_mod, b_mod.reshape(depth, 1, n))


def _norm_kernel(x_ref, g_ref, *rest, eps, modulated):
    x = x_ref[...]
    y = x * lax.rsqrt(jnp.mean(x * x, axis=-1, keepdims=True) + eps) * g_ref[...]
    if modulated:
        sh_ref, sc_ref, o_ref = rest
        y = y * (1.0 + sc_ref[...]) + sh_ref[...]
    else:
        (o_ref,) = rest
    o_ref[...] = y.astype(o_ref.dtype)


def _rmsnorm(x2d, g, shift=None, scale=None, *, rows_per_batch, out_dtype):
    m, d = x2d.shape
    tm = min(512, rows_per_batch)
    tpb = rows_per_batch // tm
    modulated = shift is not None
    in_specs = [pl.BlockSpec((tm, d), lambda i: (i, 0)), pl.BlockSpec((1, d), lambda i: (0, 0))]
    args = [x2d, g.reshape(1, d)]
    if modulated:
        vec = pl.BlockSpec((None, 1, d), lambda i: (i // tpb, 0, 0))
        in_specs += [vec, vec]
        args += [shift, scale]
    return pl.pallas_call(
        functools.partial(_norm_kernel, eps=EPS, modulated=modulated),
        grid=(m // tm,),
        in_specs=in_specs,
        out_specs=pl.BlockSpec((tm, d), lambda i: (i, 0)),
        out_shape=jax.ShapeDtypeStruct((m, d), out_dtype),
        compiler_params=_cparams(1),
        name="rmsnorm_mod" if modulated else "rmsnorm",
    )(*args)


def _mm_rope_kernel(a_ref, w_ref, cos_ref, sa_ref, sb_ref, o_ref, *, n_q_tiles, n_rope_tiles):
    j = pl.program_id(1)
    acc = _dot(a_ref[...], w_ref[...])

    @pl.when(j < n_rope_tiles)
    def _():
        cos, sa, sb = cos_ref[...], sa_ref[...], sb_ref[...]
        scale = jnp.where(j < n_q_tiles, Q_PRESCALE, 1.0).astype(F32)
        for c in range(acc.shape[1] // LANES):
            xc = acc[:, c * LANES:(c + 1) * LANES]
            up = pltpu.roll(xc, LANES - 16, 1)
            down = pltpu.roll(xc, 16, 1)
            r = xc * cos + up * sa + down * sb
            o_ref[:, c * LANES:(c + 1) * LANES] = (r * scale).astype(o_ref.dtype)

    @pl.when(j >= n_rope_tiles)
    def _():
        o_ref[...] = acc.astype(o_ref.dtype)


def _in_projection(h, w, cos, sin_up, sin_down, *, rows_per_batch, qk_width):
    m, k = h.shape
    n = w.shape[1]
    tm = min(1024, rows_per_batch)
    tn = 1024
    tpb = rows_per_batch // tm
    table = pl.BlockSpec((tm, LANES), lambda i, j: (i % tpb, 0))
    return pl.pallas_call(
        functools.partial(_mm_rope_kernel, n_q_tiles=qk_width // tn, n_rope_tiles=2 * qk_width // tn),
        grid=(m // tm, n // tn),
        in_specs=[
            pl.BlockSpec((tm, k), lambda i, j: (i, 0)),
            pl.BlockSpec((k, tn), lambda i, j: (0, j)),
            table, table, table,
        ],
        out_specs=pl.BlockSpec((tm, tn), lambda i, j: (i, j)),
        out_shape=jax.ShapeDtypeStruct((m, n), BF16),
        compiler_params=_cparams(2),
        name="in_proj",
    )(h, w, cos, sin_up, sin_down)


def _mm_relu2_kernel(a_ref, w_ref, o_ref):
    acc = jnp.maximum(_dot(a_ref[...], w_ref[...]), 0.0)
    o_ref[...] = (acc * acc).astype(o_ref.dtype)


def _matmul_relu2(a, w):
    m, k = a.shape
    n = w.shape[1]
    tm = min(1024, m)
    tn = 1024
    return pl.pallas_call(
        _mm_relu2_kernel,
        grid=(m // tm, n // tn),
        in_specs=[pl.BlockSpec((tm, k), lambda i, j: (i, 0)), pl.BlockSpec((k, tn), lambda i, j: (0, j))],
        out_specs=pl.BlockSpec((tm, tn), lambda i, j: (i, j)),
        out_shape=jax.ShapeDtypeStruct((m, n), BF16),
        compiler_params=_cparams(2),
        name="mlp_in_relu2",
    )(a, w)


def _mm_resid_kernel(a_ref, w_ref, x_ref, gate_ref, o_ref):
    o_ref[...] = x_ref[...] + gate_ref[...] * _dot(a_ref[...], w_ref[...])


def _matmul_gated_residual(a, w, x2d, gate, *, rows_per_batch, name):
    m, k = a.shape
    n = w.shape[1]
    tm = min(512, rows_per_batch)
    tn = 512
    tpb = rows_per_batch // tm
    return pl.pallas_call(
        _mm_resid_kernel,
        grid=(m // tm, n // tn),
        in_specs=[
            pl.BlockSpec((tm, k), lambda i, j: (i, 0)),
            pl.BlockSpec((k, tn), lambda i, j: (0, j)),
            pl.BlockSpec((tm, tn), lambda i, j: (i, j)),
            pl.BlockSpec((None, 1, tn), lambda i, j: (i // tpb, 0, j)),
        ],
        out_specs=pl.BlockSpec((tm, tn), lambda i, j: (i, j)),
        out_shape=jax.ShapeDtypeStruct((m, n), F32),
        compiler_params=_cparams(2),
        name=name,
    )(a, w, x2d, gate)


def _attn_kernel(lam_ref, g_ref, q_ref, *refs, key_lens, tk, lambda_init):
    n_src = len(key_lens)
    k_refs = refs[0:2 * n_src:2]
    v_refs = refs[1:2 * n_src:2]
    o_ref = refs[2 * n_src]
    scratch = refs[2 * n_src + 1:]
    vt_refs = scratch[:n_src]
    rhs_ref, m_ref, l_ref, acc_ref = scratch[n_src:]
    tq = q_ref.shape[0]

    @pl.when(pl.program_id(2) == 0)
    def _():
        for v_ref, vt_ref, klen in zip(v_refs, vt_refs, key_lens):
            def tr(c, _, v_ref=v_ref, vt_ref=vt_ref):
                off = pl.multiple_of(c * tk, tk)
                vt_ref[:, pl.ds(off, tk)] = v_ref[pl.ds(off, tk), :].astype(F32).T.astype(BF16)
                return 0
            lax.fori_loop(0, klen // tk, tr, 0)

    qt = q_ref[...].astype(F32).T
    row = lax.broadcasted_iota(jnp.int32, qt.shape, 0)
    rhs_ref[0] = jnp.where(row < ATT_HD, qt, 0.0).astype(BF16)
    rhs_ref[1] = jnp.where(row >= ATT_HD, qt, 0.0).astype(BF16)
    m_ref[...] = jnp.full(m_ref.shape, NEG_BIG, F32)
    l_ref[...] = jnp.zeros(l_ref.shape, F32)
    acc_ref[...] = jnp.zeros(acc_ref.shape, F32)

    def tile(k_tile, vt_tile):
        for mp in range(2):
            s = _dot(k_tile, rhs_ref[mp])
            m_old = m_ref[mp]
            m_new = jnp.maximum(m_old, jnp.max(s, axis=0, keepdims=True))
            alpha = jnp.exp2(m_old - m_new)
            p = jnp.exp2(s - m_new)
            l_ref[mp] = alpha * l_ref[mp] + jnp.sum(p, axis=0, keepdims=True)
            acc_ref[mp] = alpha * acc_ref[mp] + _dot(vt_tile, p.astype(BF16))
            m_ref[mp] = m_new

    for k_ref, vt_ref, klen in zip(k_refs, vt_refs, key_lens):
        def body(c, _, k_ref=k_ref, vt_ref=vt_ref):
            off = pl.multiple_of(c * tk, tk)
            tile(k_ref[pl.ds(off, tk), :], vt_ref[:, pl.ds(off, tk)])
            return 0
        lax.fori_loop(0, klen // tk, body, 0)

    lv = lam_ref[...]
    lam = (jnp.exp(jnp.sum(lv[0:1] * lv[1:2], axis=-1, keepdims=True))
           - jnp.exp(jnp.sum(lv[2:3] * lv[3:4], axis=-1, keepdims=True)) + lambda_init)
    o = acc_ref[0] * (1.0 / l_ref[0]) - lam * (acc_ref[1] * (1.0 / l_ref[1]))
    o = o * lax.rsqrt(jnp.mean(o * o, axis=0, keepdims=True) + SUBLN_EPS)
    o_ref[...] = (o.T * g_ref[...] * (1.0 - lambda_init)).astype(o_ref.dtype)


def _diff_attention(lam_rows, subln_g, q_src, kv_srcs, *, batch, q_len, key_lens, lambda_init):
    h = ATT_HEADS
    tq = min(256, q_len)
    tk = 256
    nq = q_len // tq
    in_specs = [
        pl.BlockSpec((F32_SUBLANES, LANES), lambda b, hd, i: (0, 0)),
        pl.BlockSpec((1, ATT_VD), lambda b, hd, i: (0, 0)),
        pl.BlockSpec((tq, LANES), lambda b, hd, i: (b * nq + i, hd)),
    ]
    args = [lam_rows, subln_g.reshape(1, ATT_VD), q_src]
    scratch = []
    for src, klen in zip(kv_srcs, key_lens):
        in_specs += [pl.BlockSpec((klen, LANES), lambda b, hd, i: (b, h + hd)),
                     pl.BlockSpec((klen, LANES), lambda b, hd, i: (b, 2 * h + hd))]
        args += [src, src]
        scratch.append(pltpu.VMEM((ATT_VD, klen), BF16))
    scratch += [pltpu.VMEM((2, LANES, tq), BF16), pltpu.VMEM((2, 1, tq), F32),
                pltpu.VMEM((2, 1, tq), F32), pltpu.VMEM((2, ATT_VD, tq), F32)]
    return pl.pallas_call(
        functools.partial(_attn_kernel, key_lens=tuple(key_lens), tk=tk, lambda_init=lambda_init),
        grid=(batch, h, nq),
        in_specs=in_specs,
        out_specs=pl.BlockSpec((tq, ATT_VD), lambda b, hd, i: (b * nq + i, hd)),
        out_shape=jax.ShapeDtypeStruct((batch * q_len, h * ATT_VD), BF16),
        scratch_shapes=scratch,
        compiler_params=_cparams(3),
        name="diff_attention",
    )(*args)


def _gelu(x):
    return 0.5 * x * (1.0 + lax.erf(x * (2.0 ** -0.5)))


def _sgu_kernel(su_ref, sv_ref, w_ref, b_ref, g_ref, o_ref):
    for c in range(su_ref.shape[0] // SGU_CHUNK):
        rows = slice(c * SGU_CHUNK, (c + 1) * SGU_CHUNK)
        u = _gelu(su_ref[rows, :].astype(F32))
        v = _gelu(sv_ref[rows, :].astype(F32))
        v = (v * lax.rsqrt(jnp.mean(v * v, axis=-1, keepdims=True) + EPS) * g_ref[...]).astype(BF16)
        for g in range(SGU_GROUPS):
            cols = slice(g * SGU_CH, (g + 1) * SGU_CH)
            mixed = _dot(w_ref[g], v[:, cols]) + b_ref[g]
            o_ref[rows, cols] = (u[:, cols] * mixed).astype(o_ref.dtype)


def _spatial_gating(p, w_s, b_s_lanes, g_v, *, su_block, rows_per_batch):
    m = p.shape[0]
    width = SGU_GROUPS * SGU_CH
    tm = min(512, rows_per_batch)
    return pl.pallas_call(
        _sgu_kernel,
        grid=(m // tm,),
        in_specs=[
            pl.BlockSpec((tm, width), lambda i: (i, su_block)),
            pl.BlockSpec((tm, width), lambda i: (i, su_block + 1)),
            pl.BlockSpec(w_s.shape, lambda i: (0, 0, 0)),
            pl.BlockSpec(b_s_lanes.shape, lambda i: (0, 0, 0)),
            pl.BlockSpec((1, width), lambda i: (0, 0)),
        ],
        out_specs=pl.BlockSpec((tm, width), lambda i: (i, 0)),
        out_shape=jax.ShapeDtypeStruct((m, width), BF16),
        compiler_params=_cparams(1),
        name="spatial_gating",
    )(p, p, w_s, b_s_lanes, g_v.reshape(1, width))


def _dft_cos_sin(n, scale):
    j = jnp.arange(n, dtype=jnp.int32)
    ang = ((j[:, None] * j[None, :]) % n).astype(F32) * (2.0 * math.pi / n)
    return jnp.cos(ang) * scale, jnp.sin(ang) * scale


def _fourier_stage1_kernel(x_ref, k1_ref, tc_ref, ts_ref, o_ref):
    n2, r, ch = x_ref.shape
    x = x_ref[...].reshape(n2 * r, ch)
    y = _dot(k1_ref[...], x)
    br, bi = y[:n2 * r], y[n2 * r:]
    tc = tc_ref[...].reshape(n2 * r, LANES)
    ts = ts_ref[...].reshape(n2 * r, LANES)
    tc = jnp.concatenate([tc] * (ch // LANES), axis=1)
    ts = jnp.concatenate([ts] * (ch // LANES), axis=1)
    o_ref[0] = (br * tc + bi * ts).reshape(n2, r, ch).astype(o_ref.dtype)
    o_ref[1] = (bi * tc - br * ts).reshape(n2, r, ch).astype(o_ref.dtype)


def _fourier_stage2_kernel(b_ref, k3_ref, cc_ref, sc_ref, o_ref):
    _, r, n1, ch = b_ref.shape
    rhs = b_ref[...].reshape(2 * r * n1, ch)
    y = _dot(k3_ref[...], rhs)
    yr = y[:n1 * r].astype(BF16)
    yi = y[n1 * r:].astype(BF16)
    out = _dot(yr, cc_ref[...]) + _dot(yi, sc_ref[...])
    o_ref[...] = out.reshape(n1, r, ch)


def _fourier_long(p, *, batch, seq, f_block):
    n1 = LANES
    n2 = seq // n1
    ch = FOURIER_CH
    g_n = FOURIER_GROUPS
    r1 = BF16_SUBLANES
    r2 = F32_SUBLANES

    c2, s2 = _dft_cos_sin(n2, n2 ** -0.5)
    eye1 = jnp.eye(r1, dtype=F32)
    k1 = jnp.concatenate([jnp.kron(c2, eye1), jnp.kron(-s2, eye1)], axis=0).astype(BF16)

    kk = jnp.arange(n2, dtype=jnp.int32)[:, None] * jnp.arange(n1, dtype=jnp.int32)[None, :]
    ang = (kk % seq).astype(F32) * (2.0 * math.pi / seq)
    tw_c = jnp.broadcast_to(jnp.cos(ang)[:, :, None], (n2, n1, LANES))
    tw_s = jnp.broadcast_to(jnp.sin(ang)[:, :, None], (n2, n1, LANES))

    c1, s1 = _dft_cos_sin(n1, n1 ** -0.5)
    eye2 = jnp.eye(r2, dtype=F32)
    expand = lambda a: jnp.einsum("kn,jl->kjln", a, eye2).reshape(n1 * r2, r2 * n1)
    k3 = jnp.concatenate([jnp.concatenate([expand(c1), expand(s1)], axis=1),
                          jnp.concatenate([expand(-s1), expand(c1)], axis=1)], axis=0).astype(BF16)
    cc, sc = _dft_cos_sin(ch, ch ** -0.5)
    cc, sc = cc.astype(BF16), sc.astype(BF16)

    width = p.shape[1]
    p4 = p.reshape(batch, n2, n1, width)
    stage1 = pl.pallas_call(
        _fourier_stage1_kernel,
        grid=(batch, g_n, n1 // r1),
        in_specs=[
            pl.BlockSpec((None, n2, r1, ch), lambda b, g, i: (b, 0, i, f_block + g)),
            _resident(k1.shape, lambda b, g, i: (0, 0)),
            pl.BlockSpec((n2, r1, LANES), lambda b, g, i: (0, i, 0)),
            pl.BlockSpec((n2, r1, LANES), lambda b, g, i: (0, i, 0)),
        ],
        out_specs=pl.BlockSpec((None, None, 2, n2, r1, ch), lambda b, g, i: (b, g, 0, 0, i, 0)),
        out_shape=jax.ShapeDtypeStruct((batch, g_n, 2, n2, n1, ch), BF16),
        compiler_params=_cparams(3),
        name="fourier_stage1",
    )(p4, k1, tw_c, tw_s)

    out = pl.pallas_call(
        _fourier_stage2_kernel,
        grid=(batch, g_n, n2 // r2),
        in_specs=[
            pl.BlockSpec((None, None, 2, r2, n1, ch), lambda b, g, i: (b, g, 0, i, 0, 0)),
            _resident(k3.shape, lambda b, g, i: (0, 0)),
            _resident(cc.shape, lambda b, g, i: (0, 0)),
            _resident(sc.shape, lambda b, g, i: (0, 0)),
        ],
        out_specs=pl.BlockSpec((None, n1, r2, ch), lambda b, g, i: (b, 0, i, g)),
        out_shape=jax.ShapeDtypeStruct((batch, n1, n2, g_n * ch), F32),
        compiler_params=_cparams(3),
        name="fourier_stage2",
    )(stage1, k3, cc, sc)
    return out.reshape(batch * seq, g_n * ch)


def _fourier_dense_kernel(x_ref, cl_ref, sl_ref, cc_ref, sc_ref, o_ref):
    x = x_ref[...]
    a = _dot(cl_ref[...], x).astype(BF16)
    b = _dot(sl_ref[...], x).astype(BF16)
    o_ref[...] = _dot(a, cc_ref[...]) - _dot(b, sc_ref[...])


def _fourier_short(p, *, batch, seq, f_block):
    ch = FOURIER_CH
    g_n = FOURIER_GROUPS
    cl, sl = _dft_cos_sin(seq, seq ** -0.5)
    cc, sc = _dft_cos_sin(ch, ch ** -0.5)
    const = lambda a: _resident(a.shape, lambda b, g: (0, 0))
    mats = [m.astype(BF16) for m in (cl, sl, cc, sc)]
    return pl.pallas_call(
        _fourier_dense_kernel,
        grid=(batch, g_n),
        in_specs=[pl.BlockSpec((seq, ch), lambda b, g: (b, f_block + g))] + [const(m) for m in mats],
        out_specs=pl.BlockSpec((seq, ch), lambda b, g: (b, g)),
        out_shape=jax.ShapeDtypeStruct((batch * seq, g_n * ch), F32),
        compiler_params=_cparams(2),
        name="fourier_dense",
    )(p, *mats)


def _merge_kernel(att_ref, sgu_ref, four_ref, ga_ref, gg_ref, gf_ref, wa_ref, ws_ref, wf_ref, o_ref):
    y = jax.nn.sigmoid(ga_ref[...].astype(F32)) * _dot(att_ref[...], wa_ref[...])
    y = y + jax.nn.sigmoid(gg_ref[...].astype(F32)) * _dot(sgu_ref[...], ws_ref[...])
    y = y + jax.nn.sigmoid(gf_ref[...].astype(F32)) * _dot(four_ref[...].astype(BF16), wf_ref[...])
    o_ref[...] = y.astype(o_ref.dtype)


def _merge(att, sgu, four, p, w_pa, w_ps, w_pf, *, gate_block):
    m, k = att.shape
    d = w_pa.shape[1]
    tm = min(512, m)
    act = pl.BlockSpec((tm, k), lambda i: (i, 0))
    gate = lambda off: pl.BlockSpec((tm, d), lambda i: (i, gate_block + off))
    wspec = _resident((k, d), lambda i: (0, 0))
    return pl.pallas_call(
        _merge_kernel,
        grid=(m // tm,),
        in_specs=[act, act, act, gate(0), gate(1), gate(2), wspec, wspec, wspec],
        out_specs=pl.BlockSpec((tm, d), lambda i: (i, 0)),
        out_shape=jax.ShapeDtypeStruct((m, d), BF16),
        compiler_params=_cparams(1),
        name="branch_merge",
    )(att, sgu, four, p, p, p, w_pa, w_ps, w_pf)


def _rope_tables(n_tokens):
    n_rows = n_tokens // GRID_W
    row = jnp.broadcast_to(jnp.arange(n_rows, dtype=F32)[:, None], (n_rows, GRID_W)).reshape(-1)
    col = jnp.broadcast_to(jnp.arange(GRID_W, dtype=F32)[None, :], (n_rows, GRID_W)).reshape(-1)
    n_freq = ATT_HD // 4
    inv = ROPE_BASE ** (-jnp.arange(n_freq, dtype=F32) / n_freq)
    ar = row[:, None] * inv
    ac = col[:, None] * inv
    ang = jnp.concatenate([ar, ar, ac, ac] * (LANES // ATT_HD), axis=-1)
    cos, sin = jnp.cos(ang), jnp.sin(ang)
    first_half = (jnp.arange(LANES) % (ATT_HD // 2)) < (ATT_HD // 4)
    return cos, jnp.where(first_half, -sin, 0.0), jnp.where(first_half, 0.0, sin)


def _in_proj_column_order():
    qk = ATT_HEADS * ATT_HD
    idx = []
    for base in (0, 2 * qk):
        for hd in range(ATT_HEADS):
            idx += list(range(base + hd * ATT_HD, base + (hd + 1) * ATT_HD))
            idx += list(range(base + qk + hd * ATT_HD, base + qk + (hd + 1) * ATT_HD))
    return idx


def kernel(x, c, ctx, c_ctx, w_mod, b_mod, norm1_g, norm2_g, w_in, lambda_q1, lambda_k1, lambda_q2, lambda_k2,
           subln_g, sgu_norm_g, sgu_w, sgu_b, w_proj_att, w_proj_sgu, w_proj_fourier, w_out, w_mlp_in, w_mlp_out,
           final_g):
    batch, seq, d = x.shape
    ctx_len = ctx.shape[1]
    depth = w_mod.shape[0]
    in_width = w_in.shape[2]
    qk_width = 2 * ATT_HEADS * ATT_HD
    v_width = ATT_HEADS * ATT_VD
    sgu_width = SGU_GROUPS * SGU_CH
    four_width = FOURIER_GROUPS * FOURIER_CH
    su_block = (2 * qk_width + v_width) // sgu_width
    f_block = (2 * qk_width + v_width + 2 * sgu_width) // FOURIER_CH
    gate_block = (2 * qk_width + v_width + 2 * sgu_width + four_width) // d
    assert in_width == 2 * qk_width + v_width + 2 * sgu_width + four_width + 3 * d
    assert seq % (LANES * F32_SUBLANES) == 0 and seq % GRID_W == 0 and ctx_len % 256 == 0

    perm = jnp.asarray(_in_proj_column_order() + list(range(2 * qk_width, in_width)), dtype=jnp.int32)
    w_in_b = jnp.take(w_in, perm, axis=2).astype(BF16)
    w_pa, w_ps, w_pf = (w.astype(BF16) for w in (w_proj_att, w_proj_sgu, w_proj_fourier))
    w_o, w_1, w_2 = (w.astype(BF16) for w in (w_out, w_mlp_in, w_mlp_out))
    sgu_w_b = sgu_w.astype(BF16)
    sgu_b_lanes = jnp.broadcast_to(sgu_b[..., None], sgu_b.shape + (SGU_CH,))

    cos, sin_up, sin_down = _rope_tables(seq)
    ctx_rows = min(1024, ctx_len)
    one_t, zero_t = jnp.ones((ctx_rows, LANES), F32), jnp.zeros((ctx_rows, LANES), F32)

    pad = jnp.zeros((F32_SUBLANES - batch - 1, d), F32)
    c_rows = jnp.concatenate([c, c_ctx[None, :], pad], axis=0)
    mod = _modulation(c_rows, w_mod, b_mod)

    xs =x.reshape(batch * seq, d)
    cs = ctx.reshape(batch * ctx_len, d)
    for l in range(depth):
        last = l == depth - 1
        lambda_init = 0.8 - 0.6 * math.exp(-0.3 * l)
        lam_rows = jnp.stack([lambda_q1[l], lambda_k1[l], lambda_q2[l], lambda_k2[l]]).astype(F32)
        lam_rows = jnp.pad(lam_rows, ((0, F32_SUBLANES - 4), (0, LANES - ATT_HD)))
        mod_x = mod[l, :batch].reshape(batch, 1, N_MOD, d)
        mod_c = mod[l, batch:batch + 1].reshape(1, 1, N_MOD, d)
        mx = [mod_x[:, :, i] for i in range(N_MOD)]
        mc = [mod_c[:, :, i] for i in range(N_MOD)]

        hx = _rmsnorm(xs, norm1_g[l], mx[0], mx[1], rows_per_batch=seq, out_dtype=BF16)
        hc = _rmsnorm(cs, norm1_g[l], mc[0], mc[1], rows_per_batch=batch * ctx_len, out_dtype=BF16)
        px = _in_projection(hx, w_in_b[l], cos, sin_up, sin_down, rows_per_batch=seq, qk_width=qk_width)
        pc = _in_projection(hc, w_in_b[l], one_t, zero_t, zero_t, rows_per_batch=ctx_len, qk_width=qk_width)

        att_x = _diff_attention(lam_rows, subln_g[l], px, [px, pc], batch=batch, q_len=seq,
                                key_lens=[seq, ctx_len], lambda_init=lambda_init)
        sgu_x = _spatial_gating(px, sgu_w_b[l], sgu_b_lanes[l], sgu_norm_g[l], su_block=su_block,
                                rows_per_batch=seq)
        four_x = _fourier_long(px, batch=batch, seq=seq, f_block=f_block)
        mix_x = _merge(att_x, sgu_x, four_x, px, w_pa[l], w_ps[l], w_pf[l], gate_block=gate_block)

        if not last:
            att_c = _diff_attention(lam_rows, subln_g[l], pc, [pc], batch=batch, q_len=ctx_len,
                                    key_lens=[ctx_len], lambda_init=lambda_init)
            sgu_c = _spatial_gating(pc, sgu_w_b[l], sgu_b_lanes[l], sgu_norm_g[l], su_block=su_block,
                                    rows_per_batch=ctx_len)
            four_c = _fourier_short(pc, batch=batch, seq=ctx_len, f_block=f_block)
            mix_c = _merge(att_c, sgu_c, four_c, pc, w_pa[l], w_ps[l], w_pf[l], gate_block=gate_block)
            cs = _matmul_gated_residual(mix_c, w_o[l], cs, mc[2], rows_per_batch=batch * ctx_len,
                                        name="out_proj_residual")
            hc2 = _rmsnorm(cs, norm2_g[l], mc[3], mc[4], rows_per_batch=batch * ctx_len, out_dtype=BF16)
            cs = _matmul_gated_residual(_matmul_relu2(hc2, w_1[l]), w_2[l], cs, mc[5],
                                        rows_per_batch=batch * ctx_len, name="mlp_out_residual")

        xs = _matmul_gated_residual(mix_x, w_o[l], xs, mx[2], rows_per_batch=seq, name="out_proj_residual")
        hx2 = _rmsnorm(xs, norm2_g[l], mx[3], mx[4], rows_per_batch=seq, out_dtype=BF16)
        xs = _matmul_gated_residual(_matmul_relu2(hx2, w_1[l]), w_2[l], xs, mx[5], rows_per_batch=seq,
                                    name="mlp_out_residual")

    out = _rmsnorm(xs, final_g, rows_per_batch=seq, out_dtype=x.dtype)
    return out.reshape(batch, seq, d)
```

```python
import functools
import math

import jax
import jax.numpy as jnp
from jax import lax
from jax.experimental import pallas as pl
from jax.experimental.pallas import tpu as pltpu

F32 = jnp.float32
BF16 = jnp.bfloat16

GRID_W = 64
ATT_HEADS = 8
ATT_HD = 64
ATT_VD = 2 * ATT_HD
ROPE_BASE = 10000.0
SGU_GROUPS = 8
SGU_CH = 128
SGU_CHUNK = 128
FOURIER_GROUPS = 4
FOURIER_CH = 256
N_MOD = 6
EPS = 1e-6
SUBLN_EPS = 1e-5

LANES = 128
BF16_SUBLANES = 16
F32_SUBLANES = 8
VMEM_LIMIT_BYTES = 56 * 2**20

LOG2E = 1.4426950408889634
Q_PRESCALE = (ATT_HD ** -0.5) * LOG2E
NEG_BIG = -1e30
MAX_SLAB = 64


def _cparams(n_axes):
    return pltpu.CompilerParams(dimension_semantics=("arbitrary",) * n_axes,
                                vmem_limit_bytes=VMEM_LIMIT_BYTES)


def _dot(a, b):
    return jnp.dot(a, b, preferred_element_type=F32)


def _resident(shape, index_map):
    return pl.BlockSpec(shape, index_map, pipeline_mode=pl.Buffered(1))


def _mod_kernel(c_ref, w_ref, b_ref, o_ref):
    a = c_ref[...]
    a = a * jax.nn.sigmoid(a)
    w = w_ref[...]
    a_hi = a.astype(BF16)
    a_lo = (a - a_hi.astype(F32)).astype(BF16)
    w_hi = w.astype(BF16)
    w_lo = (w - w_hi.astype(F32)).astype(BF16)
    o_ref[...] = _dot(a_hi, w_hi) + _dot(a_hi, w_lo) + _dot(a_lo, w_hi) + b_ref[...]


def _modulation(c_rows, w_mod, b_mod):
    depth, d, n = w_mod.shape
    tn = 512
    rows = c_rows.shape[0]
    return pl.pallas_call(
        _mod_kernel,
        grid=(depth, n // tn),
        in_specs=[
            pl.BlockSpec((rows, d), lambda l, j: (0, 0)),
            pl.BlockSpec((None, d, tn), lambda l, j: (l, 0, j)),
            pl.BlockSpec((None, 1, tn), lambda l, j: (l, 0, j)),
        ],
        out_specs=pl.BlockSpec((None, rows, tn), lambda l, j: (l, 0, j)),
        out_shape=jax.ShapeDtypeStruct((depth, rows, n), F32),
        compiler_params=_cparams(2),
        name="adaln_mod",
    )(c_rows, w_mod, b_mod.reshape(depth, 1, n))


def _norm_kernel(x_ref, g_ref, *rest, eps, modulated):
    x = x_ref[...]
    y = x * lax.rsqrt(jnp.mean(x * x, axis=-1, keepdims=True) + eps) * g_ref[...]
    if modulated:
        sh_ref, sc_ref, o_ref = rest
        y = y * (1.0 + sc_ref[...]) + sh_ref[...]
    else:
        (o_ref,) = rest
    o_ref[...] = y.astype(o_ref.dtype)


def _rmsnorm(x2d, g, shift=None, scale=None, *, rows_per_batch, out_dtype):
    m, d = x2d.shape
    tm = min(512, rows_per_batch)
    tpb = rows_per_batch // tm
    modulated = shift is not None
    in_specs = [pl.BlockSpec((tm, d), lambda i: (i, 0)), pl.BlockSpec((1, d), lambda i: (0, 0))]
    args = [x2d, g.reshape(1, d)]
    if modulated:
        vec = pl.BlockSpec((None, 1, d), lambda i: (i // tpb, 0, 0))
        in_specs += [vec, vec]
        args += [shift, scale]
    return pl.pallas_call(
        functools.partial(_norm_kernel, eps=EPS, modulated=modulated),
        grid=(m // tm,),
        in_specs=in_specs,
        out_specs=pl.BlockSpec((tm, d), lambda i: (i, 0)),
        out_shape=jax.ShapeDtypeStruct((m, d), out_dtype),
        compiler_params=_cparams(1),
        name="rmsnorm_mod" if modulated else "rmsnorm",
    )(*args)


def _mm_rope_kernel(a_ref, w_ref, cos_ref, sa_ref, sb_ref, o_ref, *, n_q_tiles, n_rope_tiles):
    j = pl.program_id(1)
    acc = _dot(a_ref[...], w_ref[...])

    @pl.when(j < n_rope_tiles)
    def _():
        cos, sa, sb = cos_ref[...], sa_ref[...], sb_ref[...]
        scale = jnp.where(j < n_q_tiles, Q_PRESCALE, 1.0).astype(F32)
        for c in range(acc.shape[1] // LANES):
            xc = acc[:, c * LANES:(c + 1) * LANES]
            up = pltpu.roll(xc, LANES - 16, 1)
            down = pltpu.roll(xc, 16, 1)
            r = xc * cos + up * sa + down * sb
            o_ref[:, c * LANES:(c + 1) * LANES] = (r * scale).astype(o_ref.dtype)

    @pl.when(j >= n_rope_tiles)
    def _():
        o_ref[...] = acc.astype(o_ref.dtype)


def _in_projection(h, w, cos, sin_up, sin_down, *, rows_per_batch, qk_width):
    m, k = h.shape
    n = w.shape[1]
    tm = min(1024, rows_per_batch)
    tn = 1024
    tpb = rows_per_batch // tm
    table = pl.BlockSpec((tm, LANES), lambda i, j: (i % tpb, 0))
    return pl.pallas_call(
        functools.partial(_mm_rope_kernel, n_q_tiles=qk_width // tn, n_rope_tiles=2 * qk_width // tn),
        grid=(m // tm, n // tn),
        in_specs=[
            pl.BlockSpec((tm, k), lambda i, j: (i, 0)),
            pl.BlockSpec((k, tn), lambda i, j: (0, j)),
            table, table, table,
        ],
        out_specs=pl.BlockSpec((tm, tn), lambda i, j: (i, j)),
        out_shape=jax.ShapeDtypeStruct((m, n), BF16),
        compiler_params=_cparams(2),
        name="in_proj",
    )(h, w, cos, sin_up, sin_down)


def _mm_relu2_kernel(a_ref, w_ref, o_ref):
    acc = jnp.maximum(_dot(a_ref[...], w_ref[...]), 0.0)
    o_ref[...] = (acc * acc).astype(o_ref.dtype)


def _matmul_relu2(a, w):
    m, k = a.shape
    n = w.shape[1]
    tm = min(1024, m)
    tn = 1024
    return pl.pallas_call(
        _mm_relu2_kernel,
        grid=(m // tm, n // tn),
        in_specs=[pl.BlockSpec((tm, k), lambda i, j: (i, 0)), pl.BlockSpec((k, tn), lambda i, j: (0, j))],
        out_specs=pl.BlockSpec((tm, tn), lambda i, j: (i, j)),
        out_shape=jax.ShapeDtypeStruct((m, n), BF16),
        compiler_params=_cparams(2),
        name="mlp_in_relu2",
    )(a, w)


def _mm_resid_kernel(a_ref, w_ref, x_ref, gate_ref, o_ref):
    o_ref[...] = x_ref[...] + gate_ref[...] * _dot(a_ref[...], w_ref[...])


def _matmul_gated_residual(a, w, x2d, gate, *, rows_per_batch, name):
    m, k = a.shape
    n = w.shape[1]
    tm = min(512, rows_per_batch)
    tn = 512
    tpb = rows_per_batch // tm
    return pl.pallas_call(
        _mm_resid_kernel,
        grid=(m // tm, n // tn),
        in_specs=[
            pl.BlockSpec((tm, k), lambda i, j: (i, 0)),
            pl.BlockSpec((k, tn), lambda i, j: (0, j)),
            pl.BlockSpec((tm, tn), lambda i, j: (i, j)),
            pl.BlockSpec((None, 1, tn), lambda i, j: (i // tpb, 0, j)),
        ],
        out_specs=pl.BlockSpec((tm, tn), lambda i, j: (i, j)),
        out_shape=jax.ShapeDtypeStruct((m, n), F32),
        compiler_params=_cparams(2),
        name=name,
    )(a, w, x2d, gate)


def _attn_kernel(lam_ref, g_ref, q_ref, *refs, key_lens, tk, lambda_init):
    n_src = len(key_lens)
    k_refs = refs[0:2 * n_src:2]
    v_refs = refs[1:2 * n_src:2]
    o_ref = refs[2 * n_src]
    k_all, vt_all, rhs_ref, s0_ref, s1_ref, m_ref, acc_ref = refs[2 * n_src + 1:]
    tq = q_ref.shape[0]
    n_keys = sum(key_lens)
    n_tiles = n_keys // tk
    chunk = 256

    @pl.when(pl.program_id(2) == 0)
    def _():
        base = 0
        for k_ref, v_ref, klen in zip(k_refs, v_refs, key_lens):
            def cp(c, _, k_ref=k_ref, v_ref=v_ref, base=base):
                off = pl.multiple_of(c * chunk, chunk)
                k_all[pl.ds(base + off, chunk), :] = k_ref[pl.ds(off, chunk), :]
                vt_all[0:ATT_VD, pl.ds(base + off, chunk)] = (
                    v_ref[pl.ds(off, chunk), :].astype(F32).T.astype(BF16))
                return 0
            lax.fori_loop(0, klen // chunk, cp, 0)
            base += klen
        vt_all[ATT_VD:, :] = jnp.ones((vt_all.shape[0] - ATT_VD, n_keys), BF16)

    qt = q_ref[...].astype(F32).T
    row = lax.broadcasted_iota(jnp.int32, qt.shape, 0)
    rhs_ref[0] = jnp.where(row < ATT_HD, qt, 0.0).astype(BF16)
    rhs_ref[1] = jnp.where(row >= ATT_HD, qt, 0.0).astype(BF16)
    m_ref[...] = jnp.full(m_ref.shape, NEG_BIG, F32)
    acc_ref[...] = jnp.zeros(acc_ref.shape, F32)

    def scores(t, s_ref):
        off = pl.multiple_of(t * tk, tk)
        k_tile = k_all[pl.ds(off, tk), :]
        for mp in range(2):
            s_ref[mp] = _dot(k_tile, rhs_ref[mp])

    def consume(t, s_ref):
        off = pl.multiple_of(t * tk, tk)
        vt_tile = vt_all[:, pl.ds(off, tk)]
        for mp in range(2):
            s = s_ref[mp]
            slab = jnp.max(s.reshape(tk // MAX_SLAB, MAX_SLAB, tq), axis=0)
            m_old = m_ref[mp]
            m_new = jnp.maximum(m_old, jnp.max(slab, axis=0, keepdims=True))
            alpha = jnp.exp2(m_old - m_new)
            p = jnp.exp2(s - m_new).astype(BF16)
            acc_ref[mp] = alpha * acc_ref[mp] + _dot(vt_tile, p)
            m_ref[mp] = m_new

    scores(0, s0_ref)

    def body(i, _):
        scores(2 * i + 1, s1_ref)
        consume(2 * i, s0_ref)
        scores(2 * i + 2, s0_ref)
        consume(2 * i + 1, s1_ref)
        return 0

    lax.fori_loop(0, (n_tiles - 1) // 2, body, 0)
    if n_tiles % 2 == 1:
        consume(n_tiles - 1, s0_ref)
    else:
        scores(n_tiles - 1, s1_ref)
        consume(n_tiles - 2, s0_ref)
        consume(n_tiles - 1, s1_ref)

    lv = lam_ref[...]
    lam = (jnp.exp(jnp.sum(lv[0:1] * lv[1:2], axis=-1, keepdims=True))
           - jnp.exp(jnp.sum(lv[2:3] * lv[3:4], axis=-1, keepdims=True)) + lambda_init)
    a1, a2 = acc_ref[0], acc_ref[1]
    o = (a1[:ATT_VD] * (1.0 / a1[ATT_VD:ATT_VD + 1])
         - lam * (a2[:ATT_VD] * (1.0 / a2[ATT_VD:ATT_VD + 1])))
    o = o * lax.rsqrt(jnp.mean(o * o, axis=0, keepdims=True) + SUBLN_EPS)
    o_ref[...] = (o.T * g_ref[...] * (1.0 - lambda_init)).astype(o_ref.dtype)


def _key_tile(n_keys, cap):
    best = 256
    for t in range(256, cap + 1, 256):
        if n_keys % t == 0:
            best = t
    return best


def _diff_attention(lam_rows, subln_g, q_src, kv_srcs, *, batch, q_len, key_lens, lambda_init):
    h = ATT_HEADS
    tq = min(256, q_len)
    n_keys = sum(key_lens)
    tk = _key_tile(n_keys, 1536)
    nq = q_len // tq
    v_rows = ATT_VD + BF16_SUBLANES
    in_specs = [
        pl.BlockSpec((F32_SUBLANES, LANES), lambda b, hd, i: (0, 0)),
        pl.BlockSpec((1, ATT_VD), lambda b, hd, i: (0, 0)),
        pl.BlockSpec((tq, LANES), lambda b, hd, i: (b * nq + i, hd)),
    ]
    args = [lam_rows, subln_g.reshape(1, ATT_VD), q_src]
    for src, klen in zip(kv_srcs, key_lens):
        in_specs += [pl.BlockSpec((klen, LANES), lambda b, hd, i: (b, h + hd)),
                     pl.BlockSpec((klen, LANES), lambda b, hd, i: (b, 2 * h + hd))]
        args += [src, src]
    scratch = [pltpu.VMEM((n_keys, LANES), BF16), pltpu.VMEM((v_rows, n_keys), BF16),
               pltpu.VMEM((2, LANES, tq), BF16), pltpu.VMEM((2, tk, tq), F32), pltpu.VMEM((2, tk, tq), F32),
               pltpu.VMEM((2, 1, tq), F32), pltpu.VMEM((2, v_rows, tq), F32)]
    return pl.pallas_call(
        functools.partial(_attn_kernel, key_lens=tuple(key_lens), tk=tk, lambda_init=lambda_init),
        grid=(batch, h, nq),
        in_specs=in_specs,
        out_specs=pl.BlockSpec((tq, ATT_VD), lambda b, hd, i: (b * nq + i, hd)),
        out_shape=jax.ShapeDtypeStruct((batch * q_len, h * ATT_VD), BF16),
        scratch_shapes=scratch,
        compiler_params=_cparams(3),
        name="diff_attention",
    )(*args)


def _gelu(x):
    return 0.5 * x * (1.0 + lax.erf(x * (2.0 ** -0.5)))


def _sgu_kernel(su_ref, sv_ref, w_ref, b_ref, g_ref, o_ref):
    for c in range(su_ref.shape[0] // SGU_CHUNK):
        rows = slice(c * SGU_CHUNK, (c + 1) * SGU_CHUNK)
        u = _gelu(su_ref[rows, :].astype(F32))
        v = _gelu(sv_ref[rows, :].astype(F32))
        v = (v * lax.rsqrt(jnp.mean(v * v, axis=-1, keepdims=True) + EPS) * g_ref[...]).astype(BF16)
        for g in range(SGU_GROUPS):
            cols = slice(g * SGU_CH, (g + 1) * SGU_CH)
            mixed = _dot(w_ref[g], v[:, cols]) + b_ref[g]
            o_ref[rows, cols] = (u[:, cols] * mixed).astype(o_ref.dtype)


def _spatial_gating(p, w_s, b_s_lanes, g_v, *, su_block, rows_per_batch):
    m = p.shape[0]
    width = SGU_GROUPS * SGU_CH
    tm = min(512, rows_per_batch)
    return pl.pallas_call(
        _sgu_kernel,
        grid=(m // tm,),
        in_specs=[
            pl.BlockSpec((tm, width), lambda i: (i, su_block)),
            pl.BlockSpec((tm, width), lambda i: (i, su_block + 1)),
            pl.BlockSpec(w_s.shape, lambda i: (0, 0, 0)),
            pl.BlockSpec(b_s_lanes.shape, lambda i: (0, 0, 0)),
            pl.BlockSpec((1, width), lambda i: (0, 0)),
        ],
        out_specs=pl.BlockSpec((tm, width), lambda i: (i, 0)),
        out_shape=jax.ShapeDtypeStruct((m, width), BF16),
        compiler_params=_cparams(1),
        name="spatial_gating",
    )(p, p, w_s, b_s_lanes, g_v.reshape(1, width))


def _dft_cos_sin(n, scale):
    j = jnp.arange(n, dtype=jnp.int32)
    ang = ((j[:, None] * j[None, :]) % n).astype(F32) * (2.0 * math.pi / n)
    return jnp.cos(ang) * scale, jnp.sin(ang) * scale


def _fourier_stage1_kernel(x_ref, k1_ref, tc_ref, ts_ref, o_ref):
    n2, r, ch = x_ref.shape
    x = x_ref[...].reshape(n2 * r, ch)
    y = _dot(k1_ref[...], x)
    br, bi = y[:n2 * r], y[n2 * r:]
    tc = tc_ref[...].reshape(n2 * r, LANES)
    ts = ts_ref[...].reshape(n2 * r, LANES)
    tc = jnp.concatenate([tc] * (ch // LANES), axis=1)
    ts = jnp.concatenate([ts] * (ch // LANES), axis=1)
    o_ref[0] = (br * tc + bi * ts).reshape(n2, r, ch).astype(o_ref.dtype)
    o_ref[1] = (bi * tc - br * ts).reshape(n2, r, ch).astype(o_ref.dtype)


def _fourier_stage2_kernel(b_ref, k3_ref, cc_ref, sc_ref, o_ref):
    _, r, n1, ch = b_ref.shape
    rhs = b_ref[...].reshape(2 * r * n1, ch)
    y = _dot(k3_ref[...], rhs)
    yr = y[:n1 * r].astype(BF16)
    yi = y[n1 * r:].astype(BF16)
    out = _dot(yr, cc_ref[...]) + _dot(yi, sc_ref[...])
    o_ref[...] = out.reshape(n1, r, ch)


def _fourier_long(p, *, batch, seq, f_block):
    n1 = LANES
    n2 = seq // n1
    ch = FOURIER_CH
    g_n = FOURIER_GROUPS
    r1 = BF16_SUBLANES
    r2 = F32_SUBLANES

    c2, s2 = _dft_cos_sin(n2, n2 ** -0.5)
    eye1 = jnp.eye(r1, dtype=F32)
    k1 = jnp.concatenate([jnp.kron(c2, eye1), jnp.kron(-s2, eye1)], axis=0).astype(BF16)

    kk = jnp.arange(n2, dtype=jnp.int32)[:, None] * jnp.arange(n1, dtype=jnp.int32)[None, :]
    ang = (kk % seq).astype(F32) * (2.0 * math.pi / seq)
    tw_c = jnp.broadcast_to(jnp.cos(ang)[:, :, None], (n2, n1, LANES))
    tw_s = jnp.broadcast_to(jnp.sin(ang)[:, :, None], (n2, n1, LANES))

    c1, s1 = _dft_cos_sin(n1, n1 ** -0.5)
    eye2 = jnp.eye(r2, dtype=F32)
    expand = lambda a: jnp.einsum("kn,jl->kjln", a, eye2).reshape(n1 * r2, r2 * n1)
    k3 = jnp.concatenate([jnp.concatenate([expand(c1), expand(s1)], axis=1),
                          jnp.concatenate([expand(-s1), expand(c1)], axis=1)], axis=0).astype(BF16)
    cc, sc = _dft_cos_sin(ch, ch ** -0.5)
    cc, sc = cc.astype(BF16), sc.astype(BF16)

    width = p.shape[1]
    p4 = p.reshape(batch, n2, n1, width)
    stage1 = pl.pallas_call(
        _fourier_stage1_kernel,
        grid=(batch, g_n, n1 // r1),
        in_specs=[
            pl.BlockSpec((None, n2, r1, ch), lambda b, g, i: (b, 0, i, f_block + g)),
            _resident(k1.shape, lambda b, g, i: (0, 0)),
            pl.BlockSpec((n2, r1, LANES), lambda b, g, i: (0, i, 0)),
            pl.BlockSpec((n2, r1, LANES), lambda b, g, i: (0, i, 0)),
        ],
        out_specs=pl.BlockSpec((None, None, 2, n2, r1, ch), lambda b, g, i: (b, g, 0, 0, i, 0)),
        out_shape=jax.ShapeDtypeStruct((batch, g_n, 2, n2, n1, ch), BF16),
        compiler_params=_cparams(3),
        name="fourier_stage1",
    )(p4, k1, tw_c, tw_s)

    out = pl.pallas_call(
        _fourier_stage2_kernel,
        grid=(batch, g_n, n2 // r2),
        in_specs=[
            pl.BlockSpec((None, None, 2, r2, n1, ch), lambda b, g, i: (b, g, 0, i, 0, 0)),
            _resident(k3.shape, lambda b, g, i: (0, 0)),
            _resident(cc.shape, lambda b, g, i: (0, 0)),
            _resident(sc.shape, lambda b, g, i: (0, 0)),
        ],
        out_specs=pl.BlockSpec((None, n1, r2, ch), lambda b, g, i: (b, 0, i, g)),
        out_shape=jax.ShapeDtypeStruct((batch, n1, n2, g_n * ch), F32),
        compiler_params=_cparams(3),
        name="fourier_stage2",
    )(stage1, k3, cc, sc)
    return out.reshape(batch * seq, g_n * ch)


def _fourier_dense_kernel(x_ref, cl_ref, sl_ref, cc_ref, sc_ref, o_ref):
    x = x_ref[...]
    a = _dot(cl_ref[...], x).astype(BF16)
    b = _dot(sl_ref[...], x).astype(BF16)
    o_ref[...] = _dot(a, cc_ref[...]) - _dot(b, sc_ref[...])


def _fourier_short(p, *, batch, seq, f_block):
    ch = FOURIER_CH
    g_n = FOURIER_GROUPS
    cl, sl = _dft_cos_sin(seq, seq ** -0.5)
    cc, sc = _dft_cos_sin(ch, ch ** -0.5)
    const = lambda a: _resident(a.shape, lambda b, g: (0, 0))
    mats = [m.astype(BF16) for m in (cl, sl, cc, sc)]
    return pl.pallas_call(
        _fourier_dense_kernel,
        grid=(batch, g_n),
        in_specs=[pl.BlockSpec((seq, ch), lambda b, g: (b, f_block + g))] + [const(m) for m in mats],
        out_specs=pl.BlockSpec((seq, ch), lambda b, g: (b, g)),
        out_shape=jax.ShapeDtypeStruct((batch * seq, g_n * ch), F32),
        compiler_params=_cparams(2),
        name="fourier_dense",
    )(p, *mats)


def _merge_kernel(att_ref, sgu_ref, four_ref, ga_ref, gg_ref, gf_ref, wa_ref, ws_ref, wf_ref, o_ref):
    y = jax.nn.sigmoid(ga_ref[...].astype(F32)) * _dot(att_ref[...], wa_ref[...])
    y = y + jax.nn.sigmoid(gg_ref[...].astype(F32)) * _dot(sgu_ref[...], ws_ref[...])
    y = y + jax.nn.sigmoid(gf_ref[...].astype(F32)) * _dot(four_ref[...].astype(BF16), wf_ref[...])
    o_ref[...] = y.astype(o_ref.dtype)


def _merge(att, sgu, four, p, w_pa, w_ps, w_pf, *, gate_block):
    m, k = att.shape
    d = w_pa.shape[1]
    tm = min(512, m)
    act = pl.BlockSpec((tm, k), lambda i: (i, 0))
    gate = lambda off: pl.BlockSpec((tm, d), lambda i: (i, gate_block + off))
    wspec = _resident((k, d), lambda i: (0, 0))
    return pl.pallas_call(
        _merge_kernel,
        grid=(m // tm,),
        in_specs=[act, act, act, gate(0), gate(1), gate(2), wspec, wspec, wspec],
        out_specs=pl.BlockSpec((tm, d), lambda i: (i, 0)),
        out_shape=jax.ShapeDtypeStruct((m, d), BF16),
        compiler_params=_cparams(1),
        name="branch_merge",
    )(att, sgu, four, p, p, p, w_pa, w_ps, w_pf)


def _rope_tables(n_tokens):
    n_rows = n_tokens // GRID_W
    row = jnp.broadcast_to(jnp.arange(n_rows, dtype=F32)[:, None], (n_rows, GRID_W)).reshape(-1)
    col = jnp.broadcast_to(jnp.arange(GRID_W, dtype=F32)[None, :], (n_rows, GRID_W)).reshape(-1)
    n_freq = ATT_HD // 4
    inv = ROPE_BASE ** (-jnp.arange(n_freq, dtype=F32) / n_freq)
    ar = row[:, None] * inv
    ac = col[:, None] * inv
    ang = jnp.concatenate([ar, ar, ac, ac] * (LANES // ATT_HD), axis=-1)
    cos, sin = jnp.cos(ang), jnp.sin(ang)
    first_half = (jnp.arange(LANES) % (ATT_HD // 2)) < (ATT_HD // 4)
    return cos, jnp.where(first_half, -sin, 0.0), jnp.where(first_half, 0.0, sin)


def _head_major_qk(w_in):
    depth, d, _ = w_in.shape
    qk = ATT_HEADS * ATT_HD
    seg = lambda s: w_in[:, :, s * qk:(s + 1) * qk].reshape(depth, d, ATT_HEADS, ATT_HD)
    pair = lambda a, b: jnp.concatenate([seg(a), seg(b)], axis=-1).reshape(depth, d, 2 * qk)
    return jnp.concatenate([pair(0, 1), pair(2, 3), w_in[:, :, 4 * qk:]], axis=-1)


def kernel(x, c, ctx, c_ctx, w_mod, b_mod, norm1_g, norm2_g, w_in, lambda_q1, lambda_k1, lambda_q2, lambda_k2,
           subln_g, sgu_norm_g, sgu_w, sgu_b, w_proj_att, w_proj_sgu, w_proj_fourier, w_out, w_mlp_in, w_mlp_out,
           final_g):
    batch, seq, d = x.shape
    ctx_len = ctx.shape[1]
    depth = w_mod.shape[0]
    in_width = w_in.shape[2]
    qk_width = 2 * ATT_HEADS * ATT_HD
    v_width = ATT_HEADS * ATT_VD
    sgu_width = SGU_GROUPS * SGU_CH
    four_width = FOURIER_GROUPS * FOURIER_CH
    su_block = (2 * qk_width + v_width) // sgu_width
    f_block = (2 * qk_width + v_width + 2 * sgu_width) // FOURIER_CH
    gate_block = (2 * qk_width + v_width + 2 * sgu_width + four_width) // d
    assert in_width == 2 * qk_width + v_width + 2 * sgu_width + four_width + 3 * d
    assert seq % (LANES * F32_SUBLANES) == 0 and seq % GRID_W == 0 and ctx_len % 256 == 0

    w_in_b = _head_major_qk(w_in).astype(BF16)
    w_pa, w_ps, w_pf = (w.astype(BF16) for w in (w_proj_att, w_proj_sgu, w_proj_fourier))
    w_o, w_1, w_2 = (w.astype(BF16) for w in (w_out, w_mlp_in, w_mlp_out))
    sgu_w_b = sgu_w.astype(BF16)
    sgu_b_lanes = jnp.broadcast_to(sgu_b[..., None], sgu_b.shape + (SGU_CH,))

    cos, sin_up, sin_down = _rope_tables(seq)
    ctx_rows = min(1024, ctx_len)
    one_t, zero_t = jnp.ones((ctx_rows, LANES), F32), jnp.zeros((ctx_rows, LANES), F32)

    pad = jnp.zeros((F32_SUBLANES - batch - 1, d), F32)
    c_rows = jnp.concatenate([c, c_ctx[None, :], pad], axis=0)
    mod = _modulation(c_rows, w_mod, b_mod)

    xs =x.reshape(batch * seq, d)
    cs = ctx.reshape(batch * ctx_len, d)
    for l in range(depth):
        last = l == depth - 1
        lambda_init = 0.8 - 0.6 * math.exp(-0.3 * l)
        lam_rows = jnp.stack([lambda_q1[l], lambda_k1[l], lambda_q2[l], lambda_k2[l]]).astype(F32)
        lam_rows = jnp.pad(lam_rows, ((0, F32_SUBLANES - 4), (0, LANES - ATT_HD)))
        mod_x = mod[l, :batch].reshape(batch, 1, N_MOD, d)
        mod_c = mod[l, batch:batch + 1].reshape(1, 1, N_MOD, d)
        mx = [mod_x[:, :, i] for i in range(N_MOD)]
        mc = [mod_c[:, :, i] for i in range(N_MOD)]

        hx = _rmsnorm(xs, norm1_g[l], mx[0], mx[1], rows_per_batch=seq, out_dtype=BF16)
        hc = _rmsnorm(cs, norm1_g[l], mc[0], mc[1], rows_per_batch=batch * ctx_len, out_dtype=BF16)
        px = _in_projection(hx, w_in_b[l], cos, sin_up, sin_down, rows_per_batch=seq, qk_width=qk_width)
        pc = _in_projection(hc, w_in_b[l], one_t, zero_t, zero_t, rows_per_batch=ctx_len, qk_width=qk_width)

        att_x = _diff_attention(lam_rows, subln_g[l], px, [px, pc], batch=batch, q_len=seq,
                                key_lens=[seq, ctx_len], lambda_init=lambda_init)
        sgu_x = _spatial_gating(px, sgu_w_b[l], sgu_b_lanes[l], sgu_norm_g[l], su_block=su_block,
                                rows_per_batch=seq)
        four_x = _fourier_long(px, batch=batch, seq=seq, f_block=f_block)
        mix_x = _merge(att_x, sgu_x, four_x, px, w_pa[l], w_ps[l], w_pf[l], gate_block=gate_block)

        if not last:
            att_c = _diff_attention(lam_rows, subln_g[l], pc, [pc], batch=batch, q_len=ctx_len,
                                    key_lens=[ctx_len], lambda_init=lambda_init)
            sgu_c = _spatial_gating(pc, sgu_w_b[l], sgu_b_lanes[l], sgu_norm_g[l], su_block=su_block,
                                    rows_per_batch=ctx_len)
            four_c = _fourier_short(pc, batch=batch, seq=ctx_len, f_block=f_block)
            mix_c = _merge(att_c, sgu_c, four_c, pc, w_pa[l], w_ps[l], w_pf[l], gate_block=gate_block)
            cs = _matmul_gated_residual(mix_c, w_o[l], cs, mc[2], rows_per_batch=batch * ctx_len,
                                        name="out_proj_residual")
            hc2 = _rmsnorm(cs, norm2_g[l], mc[3], mc[4], rows_per_batch=batch * ctx_len, out_dtype=BF16)
            cs = _matmul_gated_residual(_matmul_relu2(hc2, w_1[l]), w_2[l], cs, mc[5],
                                        rows_per_batch=batch * ctx_len, name="mlp_out_residual")

        xs = _matmul_gated_residual(mix_x, w_o[l], xs, mx[2], rows_per_batch=seq, name="out_proj_residual")
        hx2 = _rmsnorm(xs, norm2_g[l], mx[3], mx[4], rows_per_batch=seq, out_dtype=BF16)
        xs = _matmul_gated_residual(_matmul_relu2(hx2, w_1[l]), w_2[l], xs, mx[5], rows_per_batch=seq,
                                    name="mlp_out_residual")

    out = _rmsnorm(xs, final_g, rows_per_batch=seq, out_dtype=x.dtype)
    return out.reshape(batch, seq, d)
```

```python
import functools
import math

import jax
import jax.numpy as jnp
from jax import lax
from jax.experimental import pallas as pl
from jax.experimental.pallas import tpu as pltpu

F32 = jnp.float32
BF16 = jnp.bfloat16

GRID_W = 64
ATT_HEADS = 8
ATT_HD = 64
ATT_VD = 2 * ATT_HD
ROPE_BASE = 10000.0
SGU_GROUPS = 8
SGU_CH = 128
SGU_CHUNK = 128
FOURIER_GROUPS = 4
FOURIER_CH = 256
N_MOD = 6
EPS = 1e-6
SUBLN_EPS = 1e-5

LANES = 128
BF16_SUBLANES = 16
F32_SUBLANES = 8
VMEM_LIMIT_BYTES = 56 * 2**20

LOG2E = 1.4426950408889634
Q_PRESCALE = (ATT_HD ** -0.5) * LOG2E
NEG_BIG = -1e30
MAX_SLAB = 16


def _cparams(n_axes):
    return pltpu.CompilerParams(dimension_semantics=("arbitrary",) * n_axes,
                                vmem_limit_bytes=VMEM_LIMIT_BYTES)


def _dot(a, b):
    return jnp.dot(a, b, preferred_element_type=F32)


def _resident(shape, index_map):
    return pl.BlockSpec(shape, index_map, pipeline_mode=pl.Buffered(1))


def _mod_kernel(c_ref, w_ref, b_ref, o_ref):
    a = c_ref[...]
    a = a * jax.nn.sigmoid(a)
    w = w_ref[...]
    a_hi = a.astype(BF16)
    a_lo = (a - a_hi.astype(F32)).astype(BF16)
    w_hi = w.astype(BF16)
    w_lo = (w - w_hi.astype(F32)).astype(BF16)
    o_ref[...] = _dot(a_hi, w_hi) + _dot(a_hi, w_lo) + _dot(a_lo, w_hi) + b_ref[...]


def _modulation(c_rows, w_mod, b_mod):
    depth, d, n = w_mod.shape
    tn = 512
    rows = c_rows.shape[0]
    return pl.pallas_call(
        _mod_kernel,
        grid=(depth, n // tn),
        in_specs=[
            pl.BlockSpec((rows, d), lambda l, j: (0, 0)),
            pl.BlockSpec((None, d, tn), lambda l, j: (l, 0, j)),
            pl.BlockSpec((None, 1, tn), lambda l, j: (l, 0, j)),
        ],
        out_specs=pl.BlockSpec((None, rows, tn), lambda l, j: (l, 0, j)),
        out_shape=jax.ShapeDtypeStruct((depth, rows, n), F32),
        compiler_params=_cparams(2),
        name="adaln_mod",
    )(c_rows, w_mod, b_mod.reshape(depth, 1, n))


def _norm_kernel(x_ref, g_ref, *rest, eps, modulated):
    x = x_ref[...]
    y = x * lax.rsqrt(jnp.mean(x * x, axis=-1, keepdims=True) + eps) * g_ref[...]
    if modulated:
        sh_ref, sc_ref, o_ref = rest
        y = y * (1.0 + sc_ref[...]) + sh_ref[...]
    else:
        (o_ref,) = rest
    o_ref[...] = y.astype(o_ref.dtype)


def _rmsnorm(x2d, g, shift=None, scale=None, *, rows_per_batch, out_dtype):
    m, d = x2d.shape
    tm = min(512, rows_per_batch)
    tpb = rows_per_batch // tm
    modulated = shift is not None
    in_specs = [pl.BlockSpec((tm, d), lambda i: (i, 0)), pl.BlockSpec((1, d), lambda i: (0, 0))]
    args = [x2d, g.reshape(1, d)]
    if modulated:
        vec = pl.BlockSpec((None, 1, d), lambda i: (i // tpb, 0, 0))
        in_specs += [vec, vec]
        args += [shift, scale]
    return pl.pallas_call(
        functools.partial(_norm_kernel, eps=EPS, modulated=modulated),
        grid=(m // tm,),
        in_specs=in_specs,
        out_specs=pl.BlockSpec((tm, d), lambda i: (i, 0)),
        out_shape=jax.ShapeDtypeStruct((m, d), out_dtype),
        compiler_params=_cparams(1),
        name="rmsnorm_mod" if modulated else "rmsnorm",
    )(*args)


def _mm_rope_kernel(a_ref, w_ref, cos_ref, sa_ref, sb_ref, o_ref, *, n_q_tiles, n_rope_tiles):
    j = pl.program_id(1)
    acc = _dot(a_ref[...], w_ref[...])

    @pl.when(j < n_rope_tiles)
    def _():
        cos, sa, sb = cos_ref[...], sa_ref[...], sb_ref[...]
        scale = jnp.where(j < n_q_tiles, Q_PRESCALE, 1.0).astype(F32)
        for c in range(acc.shape[1] // LANES):
            xc = acc[:, c * LANES:(c + 1) * LANES]
            up = pltpu.roll(xc, LANES - 16, 1)
            down = pltpu.roll(xc, 16, 1)
            r = xc * cos + up * sa + down * sb
            o_ref[:, c * LANES:(c + 1) * LANES] = (r * scale).astype(o_ref.dtype)

    @pl.when(j >= n_rope_tiles)
    def _():
        o_ref[...] = acc.astype(o_ref.dtype)


def _in_projection(h, w, cos, sin_up, sin_down, *, rows_per_batch, qk_width):
    m, k = h.shape
    n = w.shape[1]
    tm = min(1024, rows_per_batch)
    tn = 1024
    tpb = rows_per_batch // tm
    table = pl.BlockSpec((tm, LANES), lambda i, j: (i % tpb, 0))
    return pl.pallas_call(
        functools.partial(_mm_rope_kernel, n_q_tiles=qk_width // tn, n_rope_tiles=2 * qk_width // tn),
        grid=(m // tm, n // tn),
        in_specs=[
            pl.BlockSpec((tm, k), lambda i, j: (i, 0)),
            pl.BlockSpec((k, tn), lambda i, j: (0, j)),
            table, table, table,
        ],
        out_specs=pl.BlockSpec((tm, tn), lambda i, j: (i, j)),
        out_shape=jax.ShapeDtypeStruct((m, n), BF16),
        compiler_params=_cparams(2),
        name="in_proj",
    )(h, w, cos, sin_up, sin_down)


def _mm_relu2_kernel(a_ref, w_ref, o_ref):
    acc = jnp.maximum(_dot(a_ref[...], w_ref[...]), 0.0)
    o_ref[...] = (acc * acc).astype(o_ref.dtype)


def _matmul_relu2(a, w):
    m, k = a.shape
    n = w.shape[1]
    tm = min(1024, m)
    tn = 1024
    return pl.pallas_call(
        _mm_relu2_kernel,
        grid=(m // tm, n // tn),
        in_specs=[pl.BlockSpec((tm, k), lambda i, j: (i, 0)), pl.BlockSpec((k, tn), lambda i, j: (0, j))],
        out_specs=pl.BlockSpec((tm, tn), lambda i, j: (i, j)),
        out_shape=jax.ShapeDtypeStruct((m, n), BF16),
        compiler_params=_cparams(2),
        name="mlp_in_relu2",
    )(a, w)


def _mm_resid_kernel(a_ref, w_ref, x_ref, gate_ref, o_ref):
    o_ref[...] = x_ref[...] + gate_ref[...] * _dot(a_ref[...], w_ref[...])


def _matmul_gated_residual(a, w, x2d, gate, *, rows_per_batch, name):
    m, k = a.shape
    n = w.shape[1]
    tm = min(512, rows_per_batch)
    tn = 512
    tpb = rows_per_batch // tm
    return pl.pallas_call(
        _mm_resid_kernel,
        grid=(m // tm, n // tn),
        in_specs=[
            pl.BlockSpec((tm, k), lambda i, j: (i, 0)),
            pl.BlockSpec((k, tn), lambda i, j: (0, j)),
            pl.BlockSpec((tm, tn), lambda i, j: (i, j)),
            pl.BlockSpec((None, 1, tn), lambda i, j: (i // tpb, 0, j)),
        ],
        out_specs=pl.BlockSpec((tm, tn), lambda i, j: (i, j)),
        out_shape=jax.ShapeDtypeStruct((m, n), F32),
        compiler_params=_cparams(2),
        name=name,
    )(a, w, x2d, gate)


def _attn_kernel(lam_ref, g_ref, q_ref, *refs, key_lens, tk, lambda_init, piece_rows, unroll_sets):
    n_src = len(key_lens)
    k_refs = refs[0:2 * n_src:2]
    v_refs = refs[1:2 * n_src:2]
    o_ref = refs[2 * n_src]
    k_all, vt_all, rhs_ref, s_ref, p_ref, mt_ref, alpha_ref, m_ref, acc_ref = refs[2 * n_src + 1:]
    tq = q_ref.shape[0]
    n_keys = sum(key_lens)
    n_tiles = n_keys // tk
    chunk = 256

    @pl.when(pl.program_id(2) == 0)
    def _():
        base = 0
        for k_ref, v_ref, klen in zip(k_refs, v_refs, key_lens):
            def cp(c, _, k_ref=k_ref, v_ref=v_ref, base=base):
                off = pl.multiple_of(c * chunk, chunk)
                k_all[pl.ds(base + off, chunk), :] = k_ref[pl.ds(off, chunk), :]
                vt_all[0:ATT_VD, pl.ds(base + off, chunk)] = (
                    v_ref[pl.ds(off, chunk), :].astype(F32).T.astype(BF16))
                return 0
            lax.fori_loop(0, klen // chunk, cp, 0)
            base += klen
        vt_all[ATT_VD:, :] = jnp.ones((vt_all.shape[0] - ATT_VD, n_keys), BF16)

    qt = q_ref[...].astype(F32).T
    row = lax.broadcasted_iota(jnp.int32, qt.shape, 0)
    rhs_ref[0] = jnp.where(row < ATT_HD, qt, 0.0).astype(BF16)
    rhs_ref[1] = jnp.where(row >= ATT_HD, qt, 0.0).astype(BF16)
    m_ref[...] = jnp.full(m_ref.shape, NEG_BIG, F32)
    stat_shape = m_ref.shape[1:]
    acc_ref[...] = jnp.zeros(acc_ref.shape, F32)

    n_chunks = tk // piece_rows

    def scores(t, par):
        off = pl.multiple_of(t * tk, tk)
        slabs = [None, None]

        def piece(c):
            rows = slice(c * piece_rows, (c + 1) * piece_rows)
            k_c = k_all[pl.ds(off + c * piece_rows, piece_rows), :]
            for mp in range(2):
                s = _dot(k_c, rhs_ref[mp])
                s_ref[par, mp, rows, :] = s
                slab = jnp.max(s.reshape(piece_rows // F32_SUBLANES, F32_SUBLANES, tq), axis=0)
                slabs[mp] = slab if slabs[mp] is None else jnp.maximum(slabs[mp], slab)

        def finish():
            for mp in range(2):
                mt_ref[par, mp] = jnp.broadcast_to(jnp.max(slabs[mp], axis=0, keepdims=True), stat_shape)

        return [functools.partial(piece, c) for c in range(n_chunks)] + [finish]

    def softmax(par):
        m_new = [None, None]

        def start():
            for mp in range(2):
                m_old = m_ref[mp]
                m_new[mp] = jnp.maximum(m_old, mt_ref[par, mp])
                alpha_ref[par, mp] = jnp.exp2(m_old - m_new[mp])
                m_ref[mp] = m_new[mp]

        def piece(c):
            rows = slice(c * piece_rows, (c + 1) * piece_rows)
            for mp in range(2):
                s = s_ref[par, mp, rows, :].reshape(piece_rows // F32_SUBLANES, F32_SUBLANES, tq)
                p_ref[par, mp, rows, :] = jnp.exp2(s - m_new[mp][None]).reshape(piece_rows, tq).astype(BF16)

        return [start] + [functools.partial(piece, c) for c in range(n_chunks)]

    def weighted_values(t, par):
        def whole():
            off = pl.multiple_of(t * tk, tk)
            vt_tile = vt_all[:, pl.ds(off, tk)]
            for mp in range(2):
                acc = acc_ref[mp].reshape(-1, F32_SUBLANES, tq) * alpha_ref[par, mp][None]
                acc_ref[mp] = acc.reshape(acc_ref.shape[1:]) + _dot(vt_tile, p_ref[par, mp])
        return [whole]

    def run(*stages):
        for k in range(max(len(s) for s in stages)):
            for s in stages:
                if k < len(s):
                    s[k]()

    def stage_set(t, par):
        run(scores(t + 2, par), softmax(1 - par), weighted_values(t, par))

    run(scores(0, 0))
    if n_tiles > 1:
        run(scores(1, 1), softmax(0))
    else:
        run(softmax(0))
    n_sets = max(n_tiles - 2, 0)

    if unroll_sets:
        for t in range(n_sets):
            stage_set(t, t % 2)
    else:
        def body(i, _):
            stage_set(2 * i, 0)
            stage_set(2 * i + 1, 1)
            return 0

        lax.fori_loop(0, n_sets // 2, body, 0)
        if n_sets % 2 == 1:
            stage_set(n_sets - 1, (n_sets - 1) % 2)
    if n_tiles > 1:
        run(weighted_values(n_tiles - 2, (n_tiles - 2) % 2), softmax((n_tiles - 1) % 2))
    run(weighted_values(n_tiles - 1, (n_tiles - 1) % 2))

    lv = lam_ref[...]
    lam = (jnp.exp(jnp.sum(lv[0:1] * lv[1:2], axis=-1, keepdims=True))
           - jnp.exp(jnp.sum(lv[2:3] * lv[3:4], axis=-1, keepdims=True)) + lambda_init)
    a1, a2 = acc_ref[0], acc_ref[1]
    o = (a1[:ATT_VD] * (1.0 / a1[ATT_VD:ATT_VD + 1])
         - lam * (a2[:ATT_VD] * (1.0 / a2[ATT_VD:ATT_VD + 1])))
    o = o * lax.rsqrt(jnp.mean(o * o, axis=0, keepdims=True) + SUBLN_EPS)
    o_ref[...] = (o.T * g_ref[...] * (1.0 - lambda_init)).astype(o_ref.dtype)


def _key_tile(n_keys, cap):
    best = 256
    for t in range(256, cap + 1, 256):
        if n_keys % t == 0:
            best = t
    return best


def _diff_attention(lam_rows, subln_g, q_src, kv_srcs, *, batch, q_len, key_lens, lambda_init):
    h = ATT_HEADS
    tq = min(256, q_len)
    n_keys = sum(key_lens)
    tk = _key_tile(n_keys, 1536)
    nq = q_len // tq
    v_rows = ATT_VD + BF16_SUBLANES
    in_specs = [
        pl.BlockSpec((F32_SUBLANES, LANES), lambda b, hd, i: (0, 0)),
        pl.BlockSpec((1, ATT_VD), lambda b, hd, i: (0, 0)),
        pl.BlockSpec((tq, LANES), lambda b, hd, i: (b * nq + i, hd)),
    ]
    args = [lam_rows, subln_g.reshape(1, ATT_VD), q_src]
    for src, klen in zip(kv_srcs, key_lens):
        in_specs += [pl.BlockSpec((klen, LANES), lambda b, hd, i: (b, h + hd)),
                     pl.BlockSpec((klen, LANES), lambda b, hd, i: (b, 2 * h + hd))]
        args += [src, src]
    scratch = [pltpu.VMEM((n_keys, LANES), BF16), pltpu.VMEM((v_rows, n_keys), BF16),
               pltpu.VMEM((2, LANES, tq), BF16), pltpu.VMEM((2, 2, tk, tq), F32),
               pltpu.VMEM((2, 2, tk, tq), BF16), pltpu.VMEM((2, 2, F32_SUBLANES, tq), F32),
               pltpu.VMEM((2, 2, F32_SUBLANES, tq), F32), pltpu.VMEM((2, F32_SUBLANES, tq), F32), pltpu.VMEM((2, v_rows, tq), F32)]
    return pl.pallas_call(
        functools.partial(_attn_kernel, key_lens=tuple(key_lens), tk=tk, lambda_init=lambda_init,
                          piece_rows=tk, unroll_sets=True),
        grid=(batch, h, nq),
        in_specs=in_specs,
        out_specs=pl.BlockSpec((tq, ATT_VD), lambda b, hd, i: (b * nq + i, hd)),
        out_shape=jax.ShapeDtypeStruct((batch * q_len, h * ATT_VD), BF16),
        scratch_shapes=scratch,
        compiler_params=_cparams(3),
        name="diff_attention",
    )(*args)


def _gelu(x):
    return 0.5 * x * (1.0 + lax.erf(x * (2.0 ** -0.5)))


def _sgu_kernel(su_ref, sv_ref, w_ref, b_ref, g_ref, o_ref):
    for c in range(su_ref.shape[0] // SGU_CHUNK):
        rows = slice(c * SGU_CHUNK, (c + 1) * SGU_CHUNK)
        u = _gelu(su_ref[rows, :].astype(F32))
        v = _gelu(sv_ref[rows, :].astype(F32))
        v = (v * lax.rsqrt(jnp.mean(v * v, axis=-1, keepdims=True) + EPS) * g_ref[...]).astype(BF16)
        for g in range(SGU_GROUPS):
            cols = slice(g * SGU_CH, (g + 1) * SGU_CH)
            mixed = _dot(w_ref[g], v[:, cols]) + b_ref[g]
            o_ref[rows, cols] = (u[:, cols] * mixed).astype(o_ref.dtype)


def _spatial_gating(p, w_s, b_s_lanes, g_v, *, su_block, rows_per_batch):
    m = p.shape[0]
    width = SGU_GROUPS * SGU_CH
    tm = min(512, rows_per_batch)
    return pl.pallas_call(
        _sgu_kernel,
        grid=(m // tm,),
        in_specs=[
            pl.BlockSpec((tm, width), lambda i: (i, su_block)),
            pl.BlockSpec((tm, width), lambda i: (i, su_block + 1)),
            pl.BlockSpec(w_s.shape, lambda i: (0, 0, 0)),
            pl.BlockSpec(b_s_lanes.shape, lambda i: (0, 0, 0)),
            pl.BlockSpec((1, width), lambda i: (0, 0)),
        ],
        out_specs=pl.BlockSpec((tm, width), lambda i: (i, 0)),
        out_shape=jax.ShapeDtypeStruct((m, width), BF16),
        compiler_params=_cparams(1),
        name="spatial_gating",
    )(p, p, w_s, b_s_lanes, g_v.reshape(1, width))


def _dft_cos_sin(n, scale):
    j = jnp.arange(n, dtype=jnp.int32)
    ang = ((j[:, None] * j[None, :]) % n).astype(F32) * (2.0 * math.pi / n)
    return jnp.cos(ang) * scale, jnp.sin(ang) * scale


def _fourier_stage1_kernel(x_ref, k1_ref, tc_ref, ts_ref, o_ref):
    n2, r, ch = x_ref.shape
    x = x_ref[...].reshape(n2 * r, ch)
    y = _dot(k1_ref[...], x)
    br, bi = y[:n2 * r], y[n2 * r:]
    tc = tc_ref[...].reshape(n2 * r, LANES)
    ts = ts_ref[...].reshape(n2 * r, LANES)
    tc = jnp.concatenate([tc] * (ch // LANES), axis=1)
    ts = jnp.concatenate([ts] * (ch // LANES), axis=1)
    o_ref[0] = (br * tc + bi * ts).reshape(n2, r, ch).astype(o_ref.dtype)
    o_ref[1] = (bi * tc - br * ts).reshape(n2, r, ch).astype(o_ref.dtype)


def _fourier_stage2_kernel(b_ref, k3_ref, cc_ref, sc_ref, o_ref):
    _, r, n1, ch = b_ref.shape
    rhs = b_ref[...].reshape(2 * r * n1, ch)
    y = _dot(k3_ref[...], rhs)
    yr = y[:n1 * r].astype(BF16)
    yi = y[n1 * r:].astype(BF16)
    out = _dot(yr, cc_ref[...]) + _dot(yi, sc_ref[...])
    o_ref[...] = out.reshape(n1, r, ch)


def _fourier_long(p, *, batch, seq, f_block):
    n1 = LANES
    n2 = seq // n1
    ch = FOURIER_CH
    g_n = FOURIER_GROUPS
    r1 = BF16_SUBLANES
    r2 = F32_SUBLANES

    c2, s2 = _dft_cos_sin(n2, n2 ** -0.5)
    eye1 = jnp.eye(r1, dtype=F32)
    k1 = jnp.concatenate([jnp.kron(c2, eye1), jnp.kron(-s2, eye1)], axis=0).astype(BF16)

    kk = jnp.arange(n2, dtype=jnp.int32)[:, None] * jnp.arange(n1, dtype=jnp.int32)[None, :]
    ang = (kk % seq).astype(F32) * (2.0 * math.pi / seq)
    tw_c = jnp.broadcast_to(jnp.cos(ang)[:, :, None], (n2, n1, LANES))
    tw_s = jnp.broadcast_to(jnp.sin(ang)[:, :, None], (n2, n1, LANES))

    c1, s1 = _dft_cos_sin(n1, n1 ** -0.5)
    eye2 = jnp.eye(r2, dtype=F32)
    expand = lambda a: jnp.einsum("kn,jl->kjln", a, eye2).reshape(n1 * r2, r2 * n1)
    k3 = jnp.concatenate([jnp.concatenate([expand(c1), expand(s1)], axis=1),
                          jnp.concatenate([expand(-s1), expand(c1)], axis=1)], axis=0).astype(BF16)
    cc, sc = _dft_cos_sin(ch, ch ** -0.5)
    cc, sc = cc.astype(BF16), sc.astype(BF16)

    width = p.shape[1]
    p4 = p.reshape(batch, n2, n1, width)
    stage1 = pl.pallas_call(
        _fourier_stage1_kernel,
        grid=(batch, g_n, n1 // r1),
        in_specs=[
            pl.BlockSpec((None, n2, r1, ch), lambda b, g, i: (b, 0, i, f_block + g)),
            _resident(k1.shape, lambda b, g, i: (0, 0)),
            pl.BlockSpec((n2, r1, LANES), lambda b, g, i: (0, i, 0)),
            pl.BlockSpec((n2, r1, LANES), lambda b, g, i: (0, i, 0)),
        ],
        out_specs=pl.BlockSpec((None, None, 2, n2, r1, ch), lambda b, g, i: (b, g, 0, 0, i, 0)),
        out_shape=jax.ShapeDtypeStruct((batch, g_n, 2, n2, n1, ch), BF16),
        compiler_params=_cparams(3),
        name="fourier_stage1",
    )(p4, k1, tw_c, tw_s)

    out = pl.pallas_call(
        _fourier_stage2_kernel,
        grid=(batch, g_n, n2 // r2),
        in_specs=[
            pl.BlockSpec((None, None, 2, r2, n1, ch), lambda b, g, i: (b, g, 0, i, 0, 0)),
            _resident(k3.shape, lambda b, g, i: (0, 0)),
            _resident(cc.shape, lambda b, g, i: (0, 0)),
            _resident(sc.shape, lambda b, g, i: (0, 0)),
        ],
        out_specs=pl.BlockSpec((None, n1, r2, ch), lambda b, g, i: (b, 0, i, g)),
        out_shape=jax.ShapeDtypeStruct((batch, n1, n2, g_n * ch), F32),
        compiler_params=_cparams(3),
        name="fourier_stage2",
    )(stage1, k3, cc, sc)
    return out.reshape(batch * seq, g_n * ch)


def _fourier_dense_kernel(x_ref, cl_ref, sl_ref, cc_ref, sc_ref, o_ref):
    x = x_ref[...]
    a = _dot(cl_ref[...], x).astype(BF16)
    b = _dot(sl_ref[...], x).astype(BF16)
    o_ref[...] = _dot(a, cc_ref[...]) - _dot(b, sc_ref[...])


def _fourier_short(p, *, batch, seq, f_block):
    ch = FOURIER_CH
    g_n = FOURIER_GROUPS
    cl, sl = _dft_cos_sin(seq, seq ** -0.5)
    cc, sc = _dft_cos_sin(ch, ch ** -0.5)
    const = lambda a: _resident(a.shape, lambda b, g: (0, 0))
    mats = [m.astype(BF16) for m in (cl, sl, cc, sc)]
    return pl.pallas_call(
        _fourier_dense_kernel,
        grid=(batch, g_n),
        in_specs=[pl.BlockSpec((seq, ch), lambda b, g: (b, f_block + g))] + [const(m) for m in mats],
        out_specs=pl.BlockSpec((seq, ch), lambda b, g: (b, g)),
        out_shape=jax.ShapeDtypeStruct((batch * seq, g_n * ch), F32),
        compiler_params=_cparams(2),
        name="fourier_dense",
    )(p, *mats)


def _merge_kernel(att_ref, sgu_ref, four_ref, ga_ref, gg_ref, gf_ref, wa_ref, ws_ref, wf_ref, o_ref):
    y = jax.nn.sigmoid(ga_ref[...].astype(F32)) * _dot(att_ref[...], wa_ref[...])
    y = y + jax.nn.sigmoid(gg_ref[...].astype(F32)) * _dot(sgu_ref[...], ws_ref[...])
    y = y + jax.nn.sigmoid(gf_ref[...].astype(F32)) * _dot(four_ref[...].astype(BF16), wf_ref[...])
    o_ref[...] = y.astype(o_ref.dtype)


def _merge(att, sgu, four, p, w_pa, w_ps, w_pf, *, gate_block):
    m, k = att.shape
    d = w_pa.shape[1]
    tm = min(512, m)
    act = pl.BlockSpec((tm, k), lambda i: (i, 0))
    gate = lambda off: pl.BlockSpec((tm, d), lambda i: (i, gate_block + off))
    wspec = _resident((k, d), lambda i: (0, 0))
    return pl.pallas_call(
        _merge_kernel,
        grid=(m // tm,),
        in_specs=[act, act, act, gate(0), gate(1), gate(2), wspec, wspec, wspec],
        out_specs=pl.BlockSpec((tm, d), lambda i: (i, 0)),
        out_shape=jax.ShapeDtypeStruct((m, d), BF16),
        compiler_params=_cparams(1),
        name="branch_merge",
    )(att, sgu, four, p, p, p, w_pa, w_ps, w_pf)


def _rope_tables(n_tokens):
    n_rows = n_tokens // GRID_W
    row = jnp.broadcast_to(jnp.arange(n_rows, dtype=F32)[:, None], (n_rows, GRID_W)).reshape(-1)
    col = jnp.broadcast_to(jnp.arange(GRID_W, dtype=F32)[None, :], (n_rows, GRID_W)).reshape(-1)
    n_freq = ATT_HD // 4
    inv = ROPE_BASE ** (-jnp.arange(n_freq, dtype=F32) / n_freq)
    ar = row[:, None] * inv
    ac = col[:, None] * inv
    ang = jnp.concatenate([ar, ar, ac, ac] * (LANES // ATT_HD), axis=-1)
    cos, sin = jnp.cos(ang), jnp.sin(ang)
    first_half = (jnp.arange(LANES) % (ATT_HD // 2)) < (ATT_HD // 4)
    return cos, jnp.where(first_half, -sin, 0.0), jnp.where(first_half, 0.0, sin)


def _head_major_qk(w_in):
    depth, d, _ = w_in.shape
    qk = ATT_HEADS * ATT_HD
    seg = lambda s: w_in[:, :, s * qk:(s + 1) * qk].reshape(depth, d, ATT_HEADS, ATT_HD)
    pair = lambda a, b: jnp.concatenate([seg(a), seg(b)], axis=-1).reshape(depth, d, 2 * qk)
    return jnp.concatenate([pair(0, 1), pair(2, 3), w_in[:, :, 4 * qk:]], axis=-1)


def kernel(x, c, ctx, c_ctx, w_mod, b_mod, norm1_g, norm2_g, w_in, lambda_q1, lambda_k1, lambda_q2, lambda_k2,
           subln_g, sgu_norm_g, sgu_w, sgu_b, w_proj_att, w_proj_sgu, w_proj_fourier, w_out, w_mlp_in, w_mlp_out,
           final_g):
    batch, seq, d = x.shape
    ctx_len = ctx.shape[1]
    depth = w_mod.shape[0]
    in_width = w_in.shape[2]
    qk_width = 2 * ATT_HEADS * ATT_HD
    v_width = ATT_HEADS * ATT_VD
    sgu_width = SGU_GROUPS * SGU_CH
    four_width = FOURIER_GROUPS * FOURIER_CH
    su_block = (2 * qk_width + v_width) // sgu_width
    f_block = (2 * qk_width + v_width + 2 * sgu_width) // FOURIER_CH
    gate_block = (2 * qk_width + v_width + 2 * sgu_width + four_width) // d
    assert in_width == 2 * qk_width + v_width + 2 * sgu_width + four_width + 3 * d
    assert seq % (LANES * F32_SUBLANES) == 0 and seq % GRID_W == 0 and ctx_len % 256 == 0

    w_in_b = _head_major_qk(w_in).astype(BF16)
    w_pa, w_ps, w_pf = (w.astype(BF16) for w in (w_proj_att, w_proj_sgu, w_proj_fourier))
    w_o, w_1, w_2 = (w.astype(BF16) for w in (w_out, w_mlp_in, w_mlp_out))
    sgu_w_b = sgu_w.astype(BF16)
    sgu_b_lanes = jnp.broadcast_to(sgu_b[..., None], sgu_b.shape + (SGU_CH,))

    cos, sin_up, sin_down = _rope_tables(seq)
    ctx_rows = min(1024, ctx_len)
    one_t, zero_t = jnp.ones((ctx_rows, LANES), F32), jnp.zeros((ctx_rows, LANES), F32)

    pad = jnp.zeros((F32_SUBLANES - batch - 1, d), F32)
    c_rows = jnp.concatenate([c, c_ctx[None, :], pad], axis=0)
    mod = _modulation(c_rows, w_mod, b_mod)

    xs =x.reshape(batch * seq, d)
    cs = ctx.reshape(batch * ctx_len, d)
    for l in range(depth):
        last = l == depth - 1
        lambda_init = 0.8 - 0.6 * math.exp(-0.3 * l)
        lam_rows = jnp.stack([lambda_q1[l], lambda_k1[l], lambda_q2[l], lambda_k2[l]]).astype(F32)
        lam_rows = jnp.pad(lam_rows, ((0, F32_SUBLANES - 4), (0, LANES - ATT_HD)))
        mod_x = mod[l, :batch].reshape(batch, 1, N_MOD, d)
        mod_c = mod[l, batch:batch + 1].reshape(1, 1, N_MOD, d)
        mx = [mod_x[:, :, i] for i in range(N_MOD)]
        mc = [mod_c[:, :, i] for i in range(N_MOD)]

        hx = _rmsnorm(xs, norm1_g[l], mx[0], mx[1], rows_per_batch=seq, out_dtype=BF16)
        hc = _rmsnorm(cs, norm1_g[l], mc[0], mc[1], rows_per_batch=batch * ctx_len, out_dtype=BF16)
        px = _in_projection(hx, w_in_b[l], cos, sin_up, sin_down, rows_per_batch=seq, qk_width=qk_width)
        pc = _in_projection(hc, w_in_b[l], one_t, zero_t, zero_t, rows_per_batch=ctx_len, qk_width=qk_width)

        att_x = _diff_attention(lam_rows, subln_g[l], px, [px, pc], batch=batch, q_len=seq,
                                key_lens=[seq, ctx_len], lambda_init=lambda_init)
        sgu_x = _spatial_gating(px, sgu_w_b[l], sgu_b_lanes[l], sgu_norm_g[l], su_block=su_block,
                                rows_per_batch=seq)
        four_x = _fourier_long(px, batch=batch, seq=seq, f_block=f_block)
        mix_x = _merge(att_x, sgu_x, four_x, px, w_pa[l], w_ps[l], w_pf[l], gate_block=gate_block)

        if not last:
            att_c = _diff_attention(lam_rows, subln_g[l], pc, [pc], batch=batch, q_len=ctx_len,
                                    key_lens=[ctx_len], lambda_init=lambda_init)
            sgu_c = _spatial_gating(pc, sgu_w_b[l], sgu_b_lanes[l], sgu_norm_g[l], su_block=su_block,
                                    rows_per_batch=ctx_len)
            four_c = _fourier_short(pc, batch=batch, seq=ctx_len, f_block=f_block)
            mix_c = _merge(att_c, sgu_c, four_c, pc, w_pa[l], w_ps[l], w_pf[l], gate_block=gate_block)
            cs = _matmul_gated_residual(mix_c, w_o[l], cs, mc[2], rows_per_batch=batch * ctx_len,
                                        name="out_proj_residual")
            hc2 = _rmsnorm(cs, norm2_g[l], mc[3], mc[4], rows_per_batch=batch * ctx_len, out_dtype=BF16)
            cs = _matmul_gated_residual(_matmul_relu2(hc2, w_1[l]), w_2[l], cs, mc[5],
                                        rows_per_batch=batch * ctx_len, name="mlp_out_residual")

        xs = _matmul_gated_residual(mix_x, w_o[l], xs, mx[2], rows_per_batch=seq, name="out_proj_residual")
        hx2 = _rmsnorm(xs, norm2_g[l], mx[3], mx[4], rows_per_batch=seq, out_dtype=BF16)
        xs = _matmul_gated_residual(_matmul_relu2(hx2, w_1[l]), w_2[l], xs, mx[5], rows_per_batch=seq,
                                    name="mlp_out_residual")

    out = _rmsnorm(xs, final_g, rows_per_batch=seq, out_dtype=x.dtype)
    return out.reshape(batch, seq, d)
```

```python
import functools
import math

import jax
import jax.numpy as jnp
from jax import lax
from jax.experimental import pallas as pl
from jax.experimental.pallas import tpu as pltpu

F32 = jnp.float32
BF16 = jnp.bfloat16

GRID_W = 64
ATT_HEADS = 8
ATT_HD = 64
ATT_VD = 2 * ATT_HD
ROPE_BASE = 10000.0
SGU_GROUPS = 8
SGU_CH = 128
SGU_CHUNK = 128
FOURIER_GROUPS = 4
FOURIER_CH = 256
N_MOD = 6
EPS = 1e-6
SUBLN_EPS = 1e-5

LANES = 128
BF16_SUBLANES = 16
F32_SUBLANES = 8
VMEM_LIMIT_BYTES = 56 * 2**20

LOG2E = 1.4426950408889634
Q_PRESCALE = (ATT_HD ** -0.5) * LOG2E
NEG_BIG = -1e30
MAX_SLAB = 64


def _cparams(n_axes):
    return pltpu.CompilerParams(dimension_semantics=("arbitrary",) * n_axes,
                                vmem_limit_bytes=VMEM_LIMIT_BYTES)


def _dot(a, b):
    return jnp.dot(a, b, preferred_element_type=F32)


def _resident(shape, index_map):
    return pl.BlockSpec(shape, index_map, pipeline_mode=pl.Buffered(1))


def _mod_kernel(c_ref, w_ref, b_ref, o_ref):
    a = c_ref[...]
    a = a * jax.nn.sigmoid(a)
    w = w_ref[...]
    a_hi = a.astype(BF16)
    a_lo = (a - a_hi.astype(F32)).astype(BF16)
    w_hi = w.astype(BF16)
    w_lo = (w - w_hi.astype(F32)).astype(BF16)
    o_ref[...] = _dot(a_hi, w_hi) + _dot(a_hi, w_lo) + _dot(a_lo, w_hi) + b_ref[...]


def _modulation(c_rows, w_mod, b_mod):
    depth, d, n = w_mod.shape
    tn = 512
    rows = c_rows.shape[0]
    return pl.pallas_call(
        _mod_kernel,
        grid=(depth, n // tn),
        in_specs=[
            pl.BlockSpec((rows, d), lambda l, j: (0, 0)),
            pl.BlockSpec((None, d, tn), lambda l, j: (l, 0, j)),
            pl.BlockSpec((None, 1, tn), lambda l, j: (l, 0, j)),
        ],
        out_specs=pl.BlockSpec((None, rows, tn), lambda l, j: (l, 0, j)),
        out_shape=jax.ShapeDtypeStruct((depth, rows, n), F32),
        compiler_params=_cparams(2),
        name="adaln_mod",
    )(c_rows, w_mod, b_mod.reshape(depth, 1, n))


def _norm_kernel(x_ref, g_ref, *rest, eps, modulated):
    x = x_ref[...]
    y = x * lax.rsqrt(jnp.mean(x * x, axis=-1, keepdims=True) + eps) * g_ref[...]
    if modulated:
        sh_ref, sc_ref, o_ref = rest
        y = y * (1.0 + sc_ref[...]) + sh_ref[...]
    else:
        (o_ref,) = rest
    o_ref[...] = y.astype(o_ref.dtype)


def _rmsnorm(x2d, g, shift=None, scale=None, *, rows_per_batch, out_dtype):
    m, d = x2d.shape
    tm = min(512, rows_per_batch)
    tpb = rows_per_batch // tm
    modulated = shift is not None
    in_specs = [pl.BlockSpec((tm, d), lambda i: (i, 0)), pl.BlockSpec((1, d), lambda i: (0, 0))]
    args = [x2d, g.reshape(1, d)]
    if modulated:
        vec = pl.BlockSpec((None, 1, d), lambda i: (i // tpb, 0, 0))
        in_specs += [vec, vec]
        args += [shift, scale]
    return pl.pallas_call(
        functools.partial(_norm_kernel, eps=EPS, modulated=modulated),
        grid=(m // tm,),
        in_specs=in_specs,
        out_specs=pl.BlockSpec((tm, d), lambda i: (i, 0)),
        out_shape=jax.ShapeDtypeStruct((m, d), out_dtype),
        compiler_params=_cparams(1),
        name="rmsnorm_mod" if modulated else "rmsnorm",
    )(*args)


def _mm_rope_kernel(a_ref, w_ref, cos_ref, sa_ref, sb_ref, o_ref, *, n_q_tiles):
    acc = _dot(a_ref[...], w_ref[...])
    cos, sa, sb = cos_ref[...], sa_ref[...], sb_ref[...]
    scale = jnp.where(pl.program_id(1) < n_q_tiles, Q_PRESCALE, 1.0).astype(F32)
    for c in range(acc.shape[1] // LANES):
        xc = acc[:, c * LANES:(c + 1) * LANES]
        up = pltpu.roll(xc, LANES - 16, 1)
        down = pltpu.roll(xc, 16, 1)
        r = xc * cos + up * sa + down * sb
        o_ref[:, c * LANES:(c + 1) * LANES] = (r * scale).astype(o_ref.dtype)


def _in_projection_qk(h, w, cos, sin_up, sin_down, *, rows_per_batch):
    m, k = h.shape
    n = w.shape[1]
    tm = min(1024, rows_per_batch)
    tn = 1024
    tpb = rows_per_batch // tm
    table = pl.BlockSpec((tm, LANES), lambda i, j: (i % tpb, 0))
    return pl.pallas_call(
        functools.partial(_mm_rope_kernel, n_q_tiles=n // (2 * tn)),
        grid=(m // tm, n // tn),
        in_specs=[
            pl.BlockSpec((tm, k), lambda i, j: (i, 0)),
            pl.BlockSpec((k, tn), lambda i, j: (0, j)),
            table, table, table,
        ],
        out_specs=pl.BlockSpec((tm, tn), lambda i, j: (i, j)),
        out_shape=jax.ShapeDtypeStruct((m, n), BF16),
        compiler_params=_cparams(2),
        name="in_proj_qk",
    )(h, w, cos, sin_up, sin_down)


def _mm_kernel(a_ref, w_ref, o_ref):
    o_ref[...] = _dot(a_ref[...], w_ref[...]).astype(o_ref.dtype)


def _in_projection_plain(h, w):
    m, k = h.shape
    n = w.shape[1]
    tm = min(1024, m)
    tn = 1024
    return pl.pallas_call(
        _mm_kernel,
        grid=(m // tm, n // tn),
        in_specs=[pl.BlockSpec((tm, k), lambda i, j: (i, 0)), pl.BlockSpec((k, tn), lambda i, j: (0, j))],
        out_specs=pl.BlockSpec((tm, tn), lambda i, j: (i, j)),
        out_shape=jax.ShapeDtypeStruct((m, n), BF16),
        compiler_params=_cparams(2),
        name="in_proj_plain",
    )(h, w)


def _in_projection_fourier(h, w, *, batch):
    m, k = h.shape
    seq = m // batch
    tm = min(1024, seq)
    tpb = seq // tm
    return pl.pallas_call(
        _mm_kernel,
        grid=(m // tm, FOURIER_GROUPS),
        in_specs=[pl.BlockSpec((tm, k), lambda i, g: (i, 0)),
                  pl.BlockSpec((k, FOURIER_CH), lambda i, g: (0, g))],
        out_specs=pl.BlockSpec((None, None, tm, FOURIER_CH), lambda i, g: (i // tpb, g, i % tpb, 0)),
        out_shape=jax.ShapeDtypeStruct((batch, FOURIER_GROUPS, seq, FOURIER_CH), BF16),
        compiler_params=_cparams(2),
        name="in_proj_fourier",
    )(h, w)


def _mm_relu2_kernel(a_ref, w_ref, o_ref):
    acc = jnp.maximum(_dot(a_ref[...], w_ref[...]), 0.0)
    o_ref[...] = (acc * acc).astype(o_ref.dtype)


def _matmul_relu2(a, w):
    m, k = a.shape
    n = w.shape[1]
    tm = min(1024, m)
    tn = 1024
    return pl.pallas_call(
        _mm_relu2_kernel,
        grid=(m // tm, n // tn),
        in_specs=[pl.BlockSpec((tm, k), lambda i, j: (i, 0)), pl.BlockSpec((k, tn), lambda i, j: (0, j))],
        out_specs=pl.BlockSpec((tm, tn), lambda i, j: (i, j)),
        out_shape=jax.ShapeDtypeStruct((m, n), BF16),
        compiler_params=_cparams(2),
        name="mlp_in_relu2",
    )(a, w)


def _mm_resid_kernel(a_ref, w_ref, x_ref, gate_ref, o_ref):
    o_ref[...] = x_ref[...] + gate_ref[...] * _dot(a_ref[...], w_ref[...])


def _matmul_gated_residual(a, w, x2d, gate, *, rows_per_batch, name):
    m, k = a.shape
    n = w.shape[1]
    tm = min(512, rows_per_batch)
    tn = 512
    tpb = rows_per_batch // tm
    return pl.pallas_call(
        _mm_resid_kernel,
        grid=(m // tm, n // tn),
        in_specs=[
            pl.BlockSpec((tm, k), lambda i, j: (i, 0)),
            pl.BlockSpec((k, tn), lambda i, j: (0, j)),
            pl.BlockSpec((tm, tn), lambda i, j: (i, j)),
            pl.BlockSpec((None, 1, tn), lambda i, j: (i // tpb, 0, j)),
        ],
        out_specs=pl.BlockSpec((tm, tn), lambda i, j: (i, j)),
        out_shape=jax.ShapeDtypeStruct((m, n), F32),
        compiler_params=_cparams(2),
        name=name,
    )(a, w, x2d, gate)


def _attn_kernel(lam_ref, g_ref, q_ref, *refs, key_lens, tk, lambda_init, piece_rows, unroll_sets):
    n_src = len(key_lens)
    k_refs = refs[0:2 * n_src:2]
    v_refs = refs[1:2 * n_src:2]
    o_ref = refs[2 * n_src]
    k_all, vt_all, rhs_ref, s_ref, p_ref, mt_ref, alpha_ref, m_ref, acc_ref = refs[2 * n_src + 1:]
    tq = q_ref.shape[0]
    n_keys = sum(key_lens)
    n_tiles = n_keys // tk
    chunk = 256

    @pl.when(pl.program_id(2) == 0)
    def _():
        base = 0
        for k_ref, v_ref, klen in zip(k_refs, v_refs, key_lens):
            def cp(c, _, k_ref=k_ref, v_ref=v_ref, base=base):
                off = pl.multiple_of(c * chunk, chunk)
                k_all[pl.ds(base + off, chunk), :] = k_ref[pl.ds(off, chunk), :]
                vt_all[0:ATT_VD, pl.ds(base + off, chunk)] = (
                    v_ref[pl.ds(off, chunk), :].astype(F32).T.astype(BF16))
                return 0
            lax.fori_loop(0, klen // chunk, cp, 0)
            base += klen
        vt_all[ATT_VD:, :] = jnp.ones((vt_all.shape[0] - ATT_VD, n_keys), BF16)

    qt = q_ref[...].astype(F32).T
    row = lax.broadcasted_iota(jnp.int32, qt.shape, 0)
    rhs_ref[0] = jnp.where(row < ATT_HD, qt, 0.0).astype(BF16)
    rhs_ref[1] = jnp.where(row >= ATT_HD, qt, 0.0).astype(BF16)
    m_ref[...] = jnp.full(m_ref.shape, NEG_BIG, F32)
    stat_shape = m_ref.shape[1:]
    acc_ref[...] = jnp.zeros(acc_ref.shape, F32)

    n_chunks = tk // piece_rows

    def scores(t, par):
        off = pl.multiple_of(t * tk, tk)
        slabs = [None, None]

        def piece(c):
            rows = slice(c * piece_rows, (c + 1) * piece_rows)
            k_c = k_all[pl.ds(off + c * piece_rows, piece_rows), :]
            for mp in range(2):
                s = _dot(k_c, rhs_ref[mp])
                s_ref[par, mp, rows, :] = s
                slab = jnp.max(s.reshape(piece_rows // MAX_SLAB, MAX_SLAB, tq), axis=0)
                slabs[mp] = slab if slabs[mp] is None else jnp.maximum(slabs[mp], slab)

        def finish():
            for mp in range(2):
                mt_ref[par, mp] = jnp.broadcast_to(jnp.max(slabs[mp], axis=0, keepdims=True), stat_shape)

        return [functools.partial(piece, c) for c in range(n_chunks)] + [finish]

    def softmax(par):
        m_new = [None, None]

        def start():
            for mp in range(2):
                m_old = m_ref[mp]
                m_new[mp] = jnp.maximum(m_old, mt_ref[par, mp])
                alpha_ref[par, mp] = jnp.exp2(m_old - m_new[mp])
                m_ref[mp] = m_new[mp]

        def piece(c):
            rows = slice(c * piece_rows, (c + 1) * piece_rows)
            for mp in range(2):
                s = s_ref[par, mp, rows, :].reshape(piece_rows // F32_SUBLANES, F32_SUBLANES, tq)
                p_ref[par, mp, rows, :] = jnp.exp2(s - m_new[mp][None]).reshape(piece_rows, tq).astype(BF16)

        return [start] + [functools.partial(piece, c) for c in range(n_chunks)]

    def weighted_values(t, par):
        def whole():
            off = pl.multiple_of(t * tk, tk)
            vt_tile = vt_all[:, pl.ds(off, tk)]
            for mp in range(2):
                acc = acc_ref[mp].reshape(-1, F32_SUBLANES, tq) * alpha_ref[par, mp][None]
                acc_ref[mp] = acc.reshape(acc_ref.shape[1:]) + _dot(vt_tile, p_ref[par, mp])
        return [whole]

    def run(*stages):
        for k in range(max(len(s) for s in stages)):
            for s in stages:
                if k < len(s):
                    s[k]()

    def stage_set(t, par):
        run(scores(t + 2, par), softmax(1 - par), weighted_values(t, par))

    run(scores(0, 0))
    if n_tiles > 1:
        run(scores(1, 1), softmax(0))
    else:
        run(softmax(0))
    n_sets = max(n_tiles - 2, 0)

    if unroll_sets:
        for t in range(n_sets):
            stage_set(t, t % 2)
    else:
        def body(i, _):
            stage_set(2 * i, 0)
            stage_set(2 * i + 1, 1)
            return 0

        lax.fori_loop(0, n_sets // 2, body, 0)
        if n_sets % 2 == 1:
            stage_set(n_sets - 1, (n_sets - 1) % 2)
    if n_tiles > 1:
        run(weighted_values(n_tiles - 2, (n_tiles - 2) % 2), softmax((n_tiles - 1) % 2))
    run(weighted_values(n_tiles - 1, (n_tiles - 1) % 2))

    lv = lam_ref[...]
    lam = (jnp.exp(jnp.sum(lv[0:1] * lv[1:2], axis=-1, keepdims=True))
           - jnp.exp(jnp.sum(lv[2:3] * lv[3:4], axis=-1, keepdims=True)) + lambda_init)
    a1, a2 = acc_ref[0], acc_ref[1]
    o = (a1[:ATT_VD] * (1.0 / a1[ATT_VD:ATT_VD + 1])
         - lam * (a2[:ATT_VD] * (1.0 / a2[ATT_VD:ATT_VD + 1])))
    o = o * lax.rsqrt(jnp.mean(o * o, axis=0, keepdims=True) + SUBLN_EPS)
    o_ref[...] = (o.T * g_ref[...] * (1.0 - lambda_init)).astype(o_ref.dtype)


def _key_tile(n_keys, cap):
    best = 256
    for t in range(256, cap + 1, 256):
        if n_keys % t == 0:
            best = t
    return best


def _diff_attention(lam_rows, subln_g, q_src, kv_srcs, *, batch, q_len, key_lens, lambda_init, v_block):
    h = ATT_HEADS
    tq = min(256, q_len)
    n_keys = sum(key_lens)
    tk = _key_tile(n_keys, 1536)
    nq = q_len // tq
    v_rows = ATT_VD + BF16_SUBLANES
    in_specs = [
        pl.BlockSpec((F32_SUBLANES, LANES), lambda b, hd, i: (0, 0)),
        pl.BlockSpec((1, ATT_VD), lambda b, hd, i: (0, 0)),
        pl.BlockSpec((tq, LANES), lambda b, hd, i: (b * nq + i, hd)),
    ]
    args = [lam_rows, subln_g.reshape(1, ATT_VD), q_src]
    for (k_src, v_src), klen in zip(kv_srcs, key_lens):
        in_specs += [pl.BlockSpec((klen, LANES), lambda b, hd, i: (b, h + hd)),
                     pl.BlockSpec((klen, LANES), lambda b, hd, i: (b, v_block + hd))]
        args += [k_src, v_src]
    scratch = [pltpu.VMEM((n_keys, LANES), BF16), pltpu.VMEM((v_rows, n_keys), BF16),
               pltpu.VMEM((2, LANES, tq), BF16), pltpu.VMEM((2, 2, tk, tq), F32),
               pltpu.VMEM((2, 2, tk, tq), BF16), pltpu.VMEM((2, 2, F32_SUBLANES, tq), F32),
               pltpu.VMEM((2, 2, F32_SUBLANES, tq), F32), pltpu.VMEM((2, F32_SUBLANES, tq), F32), pltpu.VMEM((2, v_rows, tq), F32)]
    return pl.pallas_call(
        functools.partial(_attn_kernel, key_lens=tuple(key_lens), tk=tk, lambda_init=lambda_init,
                          piece_rows=tk, unroll_sets=True),
        grid=(batch, h, nq),
        in_specs=in_specs,
        out_specs=pl.BlockSpec((tq, ATT_VD), lambda b, hd, i: (b * nq + i, hd)),
        out_shape=jax.ShapeDtypeStruct((batch * q_len, h * ATT_VD), BF16),
        scratch_shapes=scratch,
        compiler_params=_cparams(3),
        name="diff_attention",
    )(*args)


def _gelu(x):
    return 0.5 * x * (1.0 + lax.erf(x * (2.0 ** -0.5)))


def _sgu_kernel(su_ref, sv_ref, w_ref, b_ref, g_ref, o_ref):
    for c in range(su_ref.shape[0] // SGU_CHUNK):
        rows = slice(c * SGU_CHUNK, (c + 1) * SGU_CHUNK)
        u = _gelu(su_ref[rows, :].astype(F32))
        v = _gelu(sv_ref[rows, :].astype(F32))
        v = (v * lax.rsqrt(jnp.mean(v * v, axis=-1, keepdims=True) + EPS) * g_ref[...]).astype(BF16)
        for g in range(SGU_GROUPS):
            cols = slice(g * SGU_CH, (g + 1) * SGU_CH)
            mixed = _dot(w_ref[g], v[:, cols]) + b_ref[g]
            o_ref[rows, cols] = (u[:, cols] * mixed).astype(o_ref.dtype)


def _spatial_gating(p, w_s, b_s_lanes, g_v, *, su_block, rows_per_batch):
    m = p.shape[0]
    width = SGU_GROUPS * SGU_CH
    tm = min(512, rows_per_batch)
    return pl.pallas_call(
        _sgu_kernel,
        grid=(m // tm,),
        in_specs=[
            pl.BlockSpec((tm, width), lambda i: (i, su_block)),
            pl.BlockSpec((tm, width), lambda i: (i, su_block + 1)),
            pl.BlockSpec(w_s.shape, lambda i: (0, 0, 0)),
            pl.BlockSpec(b_s_lanes.shape, lambda i: (0, 0, 0)),
            pl.BlockSpec((1, width), lambda i: (0, 0)),
        ],
        out_specs=pl.BlockSpec((tm, width), lambda i: (i, 0)),
        out_shape=jax.ShapeDtypeStruct((m, width), BF16),
        compiler_params=_cparams(1),
        name="spatial_gating",
    )(p, p, w_s, b_s_lanes, g_v.reshape(1, width))


def _dft_cos_sin(n, scale):
    j = jnp.arange(n, dtype=jnp.int32)
    ang = ((j[:, None] * j[None, :]) % n).astype(F32) * (2.0 * math.pi / n)
    return jnp.cos(ang) * scale, jnp.sin(ang) * scale


def _fourier_stage1_kernel(x_ref, w_ref, tc_ref, ts_ref, o_ref):
    n2 = x_ref.shape[0]
    y = _dot(w_ref[...], x_ref[...])
    br, bi = y[:n2], y[n2:]
    reps = FOURIER_CH // LANES

    def widen(t):
        parts = [t[:, q * LANES:(q + 1) * LANES] for q in range(t.shape[1] // LANES)]
        return jnp.concatenate([p for p in parts for _ in range(reps)], axis=1)

    tc, ts = widen(tc_ref[...]), widen(ts_ref[...])
    o_ref[0] = (br * tc + bi * ts).astype(o_ref.dtype)
    o_ref[1] = (bi * tc - br * ts).astype(o_ref.dtype)


def _fourier_stage2_kernel(b_ref, k3_ref, cc_ref, sc_ref, o_ref):
    _, r, n1, ch = b_ref.shape
    rhs = b_ref[...].reshape(2 * r * n1, ch)
    y = _dot(k3_ref[...], rhs)
    yr = y[:n1 * r].astype(BF16)
    yi = y[n1 * r:].astype(BF16)
    out = _dot(yr, cc_ref[...]) + _dot(yi, sc_ref[...])
    o_ref[...] = out.reshape(n1, r, ch)


def _fourier_long(pf, *, batch, seq):
    n1 = LANES
    n2 = seq // n1
    ch = FOURIER_CH
    g_n = FOURIER_GROUPS
    r1 = min(16, n1)
    r2 = F32_SUBLANES

    c2, s2 = _dft_cos_sin(n2, n2 ** -0.5)
    w1 = jnp.concatenate([c2, -s2], axis=0).astype(BF16)

    kk = jnp.arange(n2, dtype=jnp.int32)[:, None] * jnp.arange(n1, dtype=jnp.int32)[None, :]
    ang = (kk % seq).astype(F32) * (2.0 * math.pi / seq)
    tw_c = jnp.broadcast_to(jnp.cos(ang)[:, :, None], (n2, n1, LANES)).reshape(n2, n1 * LANES)
    tw_s = jnp.broadcast_to(jnp.sin(ang)[:, :, None], (n2, n1, LANES)).reshape(n2, n1 * LANES)

    c1, s1 = _dft_cos_sin(n1, n1 ** -0.5)
    eye2 = jnp.eye(r2, dtype=F32)
    expand = lambda a: jnp.einsum("kn,jl->kjln", a, eye2).reshape(n1 * r2, r2 * n1)
    k3 = jnp.concatenate([jnp.concatenate([expand(c1), expand(s1)], axis=1),
                          jnp.concatenate([expand(-s1), expand(c1)], axis=1)], axis=0).astype(BF16)
    cc, sc = _dft_cos_sin(ch, ch ** -0.5)
    cc, sc = cc.astype(BF16), sc.astype(BF16)

    x2 = pf.reshape(batch, g_n, n2, n1 * ch)
    stage1 = pl.pallas_call(
        _fourier_stage1_kernel,
        grid=(batch, g_n, n1 // r1),
        in_specs=[
            pl.BlockSpec((None, None, n2, r1 * ch), lambda b, g, i: (b, g, 0, i)),
            _resident(w1.shape, lambda b, g, i: (0, 0)),
            pl.BlockSpec((n2, r1 * LANES), lambda b, g, i: (0, i)),
            pl.BlockSpec((n2, r1 * LANES), lambda b, g, i: (0, i)),
        ],
        out_specs=pl.BlockSpec((None, None, 2, n2, r1 * ch), lambda b, g, i: (b, g, 0, 0, i)),
        out_shape=jax.ShapeDtypeStruct((batch, g_n, 2, n2, n1 * ch), BF16),
        compiler_params=_cparams(3),
        name="fourier_stage1",
    )(x2, w1, tw_c, tw_s)
    stage1 = stage1.reshape(batch, g_n, 2, n2, n1, ch)

    out = pl.pallas_call(
        _fourier_stage2_kernel,
        grid=(batch, g_n, n2 // r2),
        in_specs=[
            pl.BlockSpec((None, None, 2, r2, n1, ch), lambda b, g, i: (b, g, 0, i, 0, 0)),
            _resident(k3.shape, lambda b, g, i: (0, 0)),
            _resident(cc.shape, lambda b, g, i: (0, 0)),
            _resident(sc.shape, lambda b, g, i: (0, 0)),
        ],
        out_specs=pl.BlockSpec((None, n1, r2, ch), lambda b, g, i: (b, 0, i, g)),
        out_shape=jax.ShapeDtypeStruct((batch, n1, n2, g_n * ch), F32),
        compiler_params=_cparams(3),
        name="fourier_stage2",
    )(stage1, k3, cc, sc)
    return out.reshape(batch * seq, g_n * ch)


def _fourier_dense_kernel(x_ref, cl_ref, sl_ref, cc_ref, sc_ref, o_ref):
    x = x_ref[...]
    a = _dot(cl_ref[...], x).astype(BF16)
    b = _dot(sl_ref[...], x).astype(BF16)
    o_ref[...] = _dot(a, cc_ref[...]) - _dot(b, sc_ref[...])


def _fourier_short(pf, *, batch, seq):
    ch = FOURIER_CH
    g_n = FOURIER_GROUPS
    cl, sl = _dft_cos_sin(seq, seq ** -0.5)
    cc, sc = _dft_cos_sin(ch, ch ** -0.5)
    const = lambda a: _resident(a.shape, lambda b, g: (0, 0))
    mats = [m.astype(BF16) for m in (cl, sl, cc, sc)]
    return pl.pallas_call(
        _fourier_dense_kernel,
        grid=(batch, g_n),
        in_specs=[pl.BlockSpec((None, None, seq, ch), lambda b, g: (b, g, 0, 0))] + [const(m) for m in mats],
        out_specs=pl.BlockSpec((seq, ch), lambda b, g: (b, g)),
        out_shape=jax.ShapeDtypeStruct((batch * seq, g_n * ch), F32),
        compiler_params=_cparams(2),
        name="fourier_dense",
    )(pf, *mats)


def _merge_kernel(att_ref, sgu_ref, four_ref, ga_ref, gg_ref, gf_ref, wa_ref, ws_ref, wf_ref, wo_ref,
                  x_ref, gate_ref, g_ref, sh_ref, sc_ref, xo_ref, ho_ref):
    y = jax.nn.sigmoid(ga_ref[...].astype(F32)) * _dot(att_ref[...], wa_ref[...])
    y = y + jax.nn.sigmoid(gg_ref[...].astype(F32)) * _dot(sgu_ref[...], ws_ref[...])
    y = y + jax.nn.sigmoid(gf_ref[...].astype(F32)) * _dot(four_ref[...].astype(BF16), wf_ref[...])
    x = x_ref[...] + gate_ref[...] * _dot(y.astype(BF16), wo_ref[...])
    xo_ref[...] = x
    h = x * lax.rsqrt(jnp.mean(x * x, axis=-1, keepdims=True) + EPS) * g_ref[...]
    ho_ref[...] = (h * (1.0 + sc_ref[...]) + sh_ref[...]).astype(ho_ref.dtype)


def _merge_project_norm(att, sgu, four, p, w_pa, w_ps, w_pf, w_o, x2d, gate, g, shift, scale, *,
                        gate_block, rows_per_batch):
    m, k = att.shape
    d = w_pa.shape[1]
    tm = min(256, rows_per_batch)
    tpb = rows_per_batch // tm
    act = pl.BlockSpec((tm, k), lambda i: (i, 0))
    gcol = lambda off: pl.BlockSpec((tm, d), lambda i: (i, gate_block + off))
    row = pl.BlockSpec((tm, d), lambda i: (i, 0))
    wspec = _resident((k, d), lambda i: (0, 0))
    vec = pl.BlockSpec((None, 1, d), lambda i: (i // tpb, 0, 0))
    return pl.pallas_call(
        _merge_kernel,
        grid=(m // tm,),
        in_specs=[act, act, act, gcol(0), gcol(1), gcol(2), wspec, wspec, wspec,
                  _resident((d, d), lambda i: (0, 0)), row, vec,
                  pl.BlockSpec((1, d), lambda i: (0, 0)), vec, vec],
        out_specs=[row, row],
        out_shape=[jax.ShapeDtypeStruct((m, d), F32), jax.ShapeDtypeStruct((m, d), BF16)],
        compiler_params=_cparams(1),
        name="merge_outproj_norm",
    )(att, sgu, four, p, p, p, w_pa, w_ps, w_pf, w_o, x2d, gate, g.reshape(1, d), shift, scale)


def _rope_tables(n_tokens):
    n_rows = n_tokens // GRID_W
    row = jnp.broadcast_to(jnp.arange(n_rows, dtype=F32)[:, None], (n_rows, GRID_W)).reshape(-1)
    col = jnp.broadcast_to(jnp.arange(GRID_W, dtype=F32)[None, :], (n_rows, GRID_W)).reshape(-1)
    n_freq = ATT_HD // 4
    inv = ROPE_BASE ** (-jnp.arange(n_freq, dtype=F32) / n_freq)
    ar = row[:, None] * inv
    ac = col[:, None] * inv
    ang = jnp.concatenate([ar, ar, ac, ac] * (LANES // ATT_HD), axis=-1)
    cos, sin = jnp.cos(ang), jnp.sin(ang)
    first_half = (jnp.arange(LANES) % (ATT_HD // 2)) < (ATT_HD // 4)
    return cos, jnp.where(first_half, -sin, 0.0), jnp.where(first_half, 0.0, sin)


def _split_in_projection(w_in):
    depth, d, _ = w_in.shape
    qk = ATT_HEADS * ATT_HD
    v0 = 4 * qk
    su0 = v0 + ATT_HEADS * ATT_VD
    f0 = su0 + 2 * SGU_GROUPS * SGU_CH
    g0 = f0 + FOURIER_GROUPS * FOURIER_CH
    seg = lambda s: w_in[:, :, s * qk:(s + 1) * qk].reshape(depth, d, ATT_HEADS, ATT_HD)
    pair = lambda a, b: jnp.concatenate([seg(a), seg(b)], axis=-1).reshape(depth, d, 2 * qk)
    w_qk = jnp.concatenate([pair(0, 1), pair(2, 3)], axis=-1)
    w_rest = jnp.concatenate([w_in[:, :, g0:], w_in[:, :, v0:f0]], axis=-1)
    return w_qk.astype(BF16), w_rest.astype(BF16), w_in[:, :, f0:g0].astype(BF16)


def kernel(x, c, ctx, c_ctx, w_mod, b_mod, norm1_g, norm2_g, w_in, lambda_q1, lambda_k1, lambda_q2, lambda_k2,
           subln_g, sgu_norm_g, sgu_w, sgu_b, w_proj_att, w_proj_sgu, w_proj_fourier, w_out, w_mlp_in, w_mlp_out,
           final_g):
    batch, seq, d = x.shape
    ctx_len = ctx.shape[1]
    depth = w_mod.shape[0]
    in_width = w_in.shape[2]
    qk_width = 2 * ATT_HEADS * ATT_HD
    v_width = ATT_HEADS * ATT_VD
    sgu_width = SGU_GROUPS * SGU_CH
    four_width = FOURIER_GROUPS * FOURIER_CH
    assert in_width == 2 * qk_width + v_width + 2 * sgu_width + four_width + 3 * d
    assert seq % (LANES * F32_SUBLANES) == 0 and seq % GRID_W == 0 and ctx_len % 256 == 0
    gate_block = 0
    v_block = 3 * d // LANES
    su_block = (3 * d + v_width) // sgu_width
    assert (3 * d + v_width) % sgu_width == 0

    w_qk, w_rest, w_four = _split_in_projection(w_in)
    w_pa, w_ps, w_pf = (w.astype(BF16) for w in (w_proj_att, w_proj_sgu, w_proj_fourier))
    w_o, w_1, w_2 = (w.astype(BF16) for w in (w_out, w_mlp_in, w_mlp_out))
    sgu_w_b = sgu_w.astype(BF16)
    sgu_b_lanes = jnp.broadcast_to(sgu_b[..., None], sgu_b.shape + (SGU_CH,))

    cos, sin_up, sin_down = _rope_tables(seq)
    ctx_rows = min(1024, ctx_len)
    one_t, zero_t = jnp.ones((ctx_rows, LANES), F32), jnp.zeros((ctx_rows, LANES), F32)

    pad = jnp.zeros((F32_SUBLANES - batch - 1, d), F32)
    c_rows = jnp.concatenate([c, c_ctx[None, :], pad], axis=0)
    mod = _modulation(c_rows, w_mod, b_mod)

    xs =x.reshape(batch * seq, d)
    cs = ctx.reshape(batch * ctx_len, d)
    for l in range(depth):
        last = l == depth - 1
        lambda_init = 0.8 - 0.6 * math.exp(-0.3 * l)
        lam_rows = jnp.stack([lambda_q1[l], lambda_k1[l], lambda_q2[l], lambda_k2[l]]).astype(F32)
        lam_rows = jnp.pad(lam_rows, ((0, F32_SUBLANES - 4), (0, LANES - ATT_HD)))
        mod_x = mod[l, :batch].reshape(batch, 1, N_MOD, d)
        mod_c = mod[l, batch:batch + 1].reshape(1, 1, N_MOD, d)
        mx = [mod_x[:, :, i] for i in range(N_MOD)]
        mc = [mod_c[:, :, i] for i in range(N_MOD)]

        hx = _rmsnorm(xs, norm1_g[l], mx[0], mx[1], rows_per_batch=seq, out_dtype=BF16)
        hc = _rmsnorm(cs, norm1_g[l], mc[0], mc[1], rows_per_batch=batch * ctx_len, out_dtype=BF16)
        qk_x = _in_projection_qk(hx, w_qk[l], cos, sin_up, sin_down, rows_per_batch=seq)
        qk_c = _in_projection_qk(hc, w_qk[l], one_t, zero_t, zero_t, rows_per_batch=ctx_len)
        px = _in_projection_plain(hx, w_rest[l])
        pc = _in_projection_plain(hc, w_rest[l])
        fx = _in_projection_fourier(hx, w_four[l], batch=batch)

        att_x = _diff_attention(lam_rows, subln_g[l], qk_x, [(qk_x, px), (qk_c, pc)], batch=batch, q_len=seq,
                                key_lens=[seq, ctx_len], lambda_init=lambda_init, v_block=v_block)
        sgu_x = _spatial_gating(px, sgu_w_b[l], sgu_b_lanes[l], sgu_norm_g[l], su_block=su_block,
                                rows_per_batch=seq)
        four_x = _fourier_long(fx, batch=batch, seq=seq)

        if not last:
            fc = _in_projection_fourier(hc, w_four[l], batch=batch)
            att_c = _diff_attention(lam_rows, subln_g[l], qk_c, [(qk_c, pc)], batch=batch, q_len=ctx_len,
                                    key_lens=[ctx_len], lambda_init=lambda_init, v_block=v_block)
            sgu_c = _spatial_gating(pc, sgu_w_b[l], sgu_b_lanes[l], sgu_norm_g[l], su_block=su_block,
                                    rows_per_batch=ctx_len)
            four_c = _fourier_short(fc, batch=batch, seq=ctx_len)
            cs, hc2 = _merge_project_norm(att_c, sgu_c, four_c, pc, w_pa[l], w_ps[l], w_pf[l], w_o[l], cs,
                                          mc[2], norm2_g[l], mc[3], mc[4], gate_block=gate_block,
                                          rows_per_batch=batch * ctx_len)
            cs = _matmul_gated_residual(_matmul_relu2(hc2, w_1[l]), w_2[l], cs, mc[5],
                                        rows_per_batch=batch * ctx_len, name="mlp_out_residual")

        xs, hx2 = _merge_project_norm(att_x, sgu_x, four_x, px, w_pa[l], w_ps[l], w_pf[l], w_o[l], xs,
                                      mx[2], norm2_g[l], mx[3], mx[4], gate_block=gate_block,
                                      rows_per_batch=seq)
        xs = _matmul_gated_residual(_matmul_relu2(hx2, w_1[l]), w_2[l], xs, mx[5], rows_per_batch=seq,
                                    name="mlp_out_residual")

    out = _rmsnorm(xs, final_g, rows_per_batch=seq, out_dtype=x.dtype)
    return out.reshape(batch, seq, d)
```

```python
import functools
import math

import jax
import jax.numpy as jnp
from jax import lax
from jax.experimental import pallas as pl
from jax.experimental.pallas import tpu as pltpu

F32 = jnp.float32
BF16 = jnp.bfloat16

GRID_W = 64
ATT_HEADS = 8
ATT_HD = 64
ATT_VD = 2 * ATT_HD
ROPE_BASE = 10000.0
SGU_GROUPS = 8
SGU_CH = 128
SGU_CHUNK = 128
FOURIER_GROUPS = 4
FOURIER_CH = 256
N_MOD = 6
EPS = 1e-6
SUBLN_EPS = 1e-5

LANES = 128
BF16_SUBLANES = 16
F32_SUBLANES = 8
VMEM_LIMIT_BYTES = 56 * 2**20

LOG2E = 1.4426950408889634
Q_PRESCALE = (ATT_HD ** -0.5) * LOG2E
NEG_BIG = -1e30
MAX_SLAB = 64


def _cparams(n_axes):
    return pltpu.CompilerParams(dimension_semantics=("arbitrary",) * n_axes,
                                vmem_limit_bytes=VMEM_LIMIT_BYTES)


def _dot(a, b):
    return jnp.dot(a, b, preferred_element_type=F32)


def _resident(shape, index_map):
    return pl.BlockSpec(shape, index_map, pipeline_mode=pl.Buffered(1))


def _mod_kernel(c_ref, w_ref, b_ref, o_ref):
    a = c_ref[...]
    a = a * jax.nn.sigmoid(a)
    w = w_ref[...]
    a_hi = a.astype(BF16)
    a_lo = (a - a_hi.astype(F32)).astype(BF16)
    w_hi = w.astype(BF16)
    w_lo = (w - w_hi.astype(F32)).astype(BF16)
    o_ref[...] = _dot(a_hi, w_hi) + _dot(a_hi, w_lo) + _dot(a_lo, w_hi) + b_ref[...]


def _modulation(c_rows, w_mod, b_mod):
    depth, d, n = w_mod.shape
    tn = 512
    rows = c_rows.shape[0]
    return pl.pallas_call(
        _mod_kernel,
        grid=(depth, n // tn),
        in_specs=[
            pl.BlockSpec((rows, d), lambda l, j: (0, 0)),
            pl.BlockSpec((None, d, tn), lambda l, j: (l, 0, j)),
            pl.BlockSpec((None, 1, tn), lambda l, j: (l, 0, j)),
        ],
        out_specs=pl.BlockSpec((None, rows, tn), lambda l, j: (l, 0, j)),
        out_shape=jax.ShapeDtypeStruct((depth, rows, n), F32),
        compiler_params=_cparams(2),
        name="adaln_mod",
    )(c_rows, w_mod, b_mod.reshape(depth, 1, n))


def _norm_kernel(x_ref, g_ref, *rest, eps, modulated):
    x = x_ref[...]
    y = x * lax.rsqrt(jnp.mean(x * x, axis=-1, keepdims=True) + eps) * g_ref[...]
    if modulated:
        sh_ref, sc_ref, o_ref = rest
        y = y * (1.0 + sc_ref[...]) + sh_ref[...]
    else:
        (o_ref,) = rest
    o_ref[...] = y.astype(o_ref.dtype)


def _rmsnorm(x2d, g, shift=None, scale=None, *, rows_per_batch, out_dtype):
    m, d = x2d.shape
    tm = min(512, rows_per_batch)
    tpb = rows_per_batch // tm
    modulated = shift is not None
    in_specs = [pl.BlockSpec((tm, d), lambda i: (i, 0)), pl.BlockSpec((1, d), lambda i: (0, 0))]
    args = [x2d, g.reshape(1, d)]
    if modulated:
        vec = pl.BlockSpec((None, 1, d), lambda i: (i // tpb, 0, 0))
        in_specs += [vec, vec]
        args += [shift, scale]
    return pl.pallas_call(
        functools.partial(_norm_kernel, eps=EPS, modulated=modulated),
        grid=(m // tm,),
        in_specs=in_specs,
        out_specs=pl.BlockSpec((tm, d), lambda i: (i, 0)),
        out_shape=jax.ShapeDtypeStruct((m, d), out_dtype),
        compiler_params=_cparams(1),
        name="rmsnorm_mod" if modulated else "rmsnorm",
    )(*args)


def _mm_rope_kernel(a_ref, w_ref, cos_ref, sa_ref, sb_ref, o_ref, *, n_q_tiles):
    acc = _dot(a_ref[...], w_ref[...])
    cos, sa, sb = cos_ref[...], sa_ref[...], sb_ref[...]
    scale = jnp.where(pl.program_id(1) < n_q_tiles, Q_PRESCALE, 1.0).astype(F32)
    for c in range(acc.shape[1] // LANES):
        xc = acc[:, c * LANES:(c + 1) * LANES]
        up = pltpu.roll(xc, LANES - 16, 1)
        down = pltpu.roll(xc, 16, 1)
        r = xc * cos + up * sa + down * sb
        o_ref[:, c * LANES:(c + 1) * LANES] = (r * scale).astype(o_ref.dtype)


def _in_projection_qk(h, w, cos, sin_up, sin_down, *, rows_per_batch):
    m, k = h.shape
    n = w.shape[1]
    tm = min(1024, rows_per_batch)
    tn = 1024
    tpb = rows_per_batch // tm
    table = pl.BlockSpec((tm, LANES), lambda i, j: (i % tpb, 0))
    return pl.pallas_call(
        functools.partial(_mm_rope_kernel, n_q_tiles=n // (2 * tn)),
        grid=(m // tm, n // tn),
        in_specs=[
            pl.BlockSpec((tm, k), lambda i, j: (i, 0)),
            pl.BlockSpec((k, tn), lambda i, j: (0, j)),
            table, table, table,
        ],
        out_specs=pl.BlockSpec((tm, tn), lambda i, j: (i, j)),
        out_shape=jax.ShapeDtypeStruct((m, n), BF16),
        compiler_params=_cparams(2),
        name="in_proj_qk",
    )(h, w, cos, sin_up, sin_down)


def _mm_kernel(a_ref, w_ref, o_ref):
    o_ref[...] = _dot(a_ref[...], w_ref[...]).astype(o_ref.dtype)


def _in_projection_plain(h, w):
    m, k = h.shape
    n = w.shape[1]
    tm = min(1024, m)
    tn = 1024
    return pl.pallas_call(
        _mm_kernel,
        grid=(m // tm, n // tn),
        in_specs=[pl.BlockSpec((tm, k), lambda i, j: (i, 0)), pl.BlockSpec((k, tn), lambda i, j: (0, j))],
        out_specs=pl.BlockSpec((tm, tn), lambda i, j: (i, j)),
        out_shape=jax.ShapeDtypeStruct((m, n), BF16),
        compiler_params=_cparams(2),
        name="in_proj_plain",
    )(h, w)


def _in_projection_fourier(h, w, *, batch):
    m, k = h.shape
    seq = m // batch
    tm = min(1024, seq)
    tpb = seq // tm
    return pl.pallas_call(
        _mm_kernel,
        grid=(m // tm, FOURIER_GROUPS),
        in_specs=[pl.BlockSpec((tm, k), lambda i, g: (i, 0)),
                  pl.BlockSpec((k, FOURIER_CH), lambda i, g: (0, g))],
        out_specs=pl.BlockSpec((None, None, tm, FOURIER_CH), lambda i, g: (i // tpb, g, i % tpb, 0)),
        out_shape=jax.ShapeDtypeStruct((batch, FOURIER_GROUPS, seq, FOURIER_CH), BF16),
        compiler_params=_cparams(2),
        name="in_proj_fourier",
    )(h, w)


def _mm_relu2_kernel(a_ref, w_ref, o_ref):
    acc = jnp.maximum(_dot(a_ref[...], w_ref[...]), 0.0)
    o_ref[...] = (acc * acc).astype(o_ref.dtype)


def _matmul_relu2(a, w):
    m, k = a.shape
    n = w.shape[1]
    tm = min(1024, m)
    tn = 1024
    return pl.pallas_call(
        _mm_relu2_kernel,
        grid=(m // tm, n // tn),
        in_specs=[pl.BlockSpec((tm, k), lambda i, j: (i, 0)), pl.BlockSpec((k, tn), lambda i, j: (0, j))],
        out_specs=pl.BlockSpec((tm, tn), lambda i, j: (i, j)),
        out_shape=jax.ShapeDtypeStruct((m, n), BF16),
        compiler_params=_cparams(2),
        name="mlp_in_relu2",
    )(a, w)


def _mm_resid_kernel(a_ref, w_ref, x_ref, gate_ref, o_ref):
    o_ref[...] = x_ref[...] + gate_ref[...] * _dot(a_ref[...], w_ref[...])


def _matmul_gated_residual(a, w, x2d, gate, *, rows_per_batch, name):
    m, k = a.shape
    n = w.shape[1]
    tm = min(1024, rows_per_batch)
    tn = 256
    tpb = rows_per_batch // tm
    return pl.pallas_call(
        _mm_resid_kernel,
        grid=(m // tm, n // tn),
        in_specs=[
            pl.BlockSpec((tm, k), lambda i, j: (i, 0)),
            pl.BlockSpec((k, tn), lambda i, j: (0, j)),
            pl.BlockSpec((tm, tn), lambda i, j: (i, j)),
            pl.BlockSpec((None, 1, tn), lambda i, j: (i // tpb, 0, j)),
        ],
        out_specs=pl.BlockSpec((tm, tn), lambda i, j: (i, j)),
        out_shape=jax.ShapeDtypeStruct((m, n), F32),
        compiler_params=_cparams(2),
        name=name,
    )(a, w, x2d, gate)


def _attn_kernel(lam_ref, g_ref, q_ref, *refs, key_lens, tk, lambda_init, piece_rows, unroll_sets):
    n_src = len(key_lens)
    k_refs = refs[0:2 * n_src:2]
    v_refs = refs[1:2 * n_src:2]
    o_ref = refs[2 * n_src]
    k_all, vt_all, rhs_ref, s_ref, p_ref, mt_ref, alpha_ref, m_ref, acc_ref = refs[2 * n_src + 1:]
    tq = q_ref.shape[0]
    n_keys = sum(key_lens)
    n_tiles = n_keys // tk
    chunk = 256

    @pl.when(pl.program_id(2) == 0)
    def _():
        base = 0
        for k_ref, v_ref, klen in zip(k_refs, v_refs, key_lens):
            def cp(c, _, k_ref=k_ref, v_ref=v_ref, base=base):
                off = pl.multiple_of(c * chunk, chunk)
                k_all[pl.ds(base + off, chunk), :] = k_ref[pl.ds(off, chunk), :]
                vt_all[0:ATT_VD, pl.ds(base + off, chunk)] = (
                    v_ref[pl.ds(off, chunk), :].astype(F32).T.astype(BF16))
                return 0
            lax.fori_loop(0, klen // chunk, cp, 0)
            base += klen
        vt_all[ATT_VD:, :] = jnp.ones((vt_all.shape[0] - ATT_VD, n_keys), BF16)

    qt = q_ref[...].astype(F32).T
    row = lax.broadcasted_iota(jnp.int32, qt.shape, 0)
    rhs_ref[0] = jnp.where(row < ATT_HD, qt, 0.0).astype(BF16)
    rhs_ref[1] = jnp.where(row >= ATT_HD, qt, 0.0).astype(BF16)
    m_ref[...] = jnp.full(m_ref.shape, NEG_BIG, F32)
    stat_shape = m_ref.shape[1:]
    acc_ref[...] = jnp.zeros(acc_ref.shape, F32)

    n_chunks = tk // piece_rows

    def scores(t, par):
        off = pl.multiple_of(t * tk, tk)
        slabs = [None, None]

        def piece(c):
            rows = slice(c * piece_rows, (c + 1) * piece_rows)
            k_c = k_all[pl.ds(off + c * piece_rows, piece_rows), :]
            for mp in range(2):
                s = _dot(k_c, rhs_ref[mp])
                s_ref[par, mp, rows, :] = s
                slab = jnp.max(s.reshape(piece_rows // MAX_SLAB, MAX_SLAB, tq), axis=0)
                slabs[mp] = slab if slabs[mp] is None else jnp.maximum(slabs[mp], slab)

        def finish():
            for mp in range(2):
                mt_ref[par, mp] = jnp.broadcast_to(jnp.max(slabs[mp], axis=0, keepdims=True), stat_shape)

        return [functools.partial(piece, c) for c in range(n_chunks)] + [finish]

    def softmax(par):
        m_new = [None, None]

        def start():
            for mp in range(2):
                m_old = m_ref[mp]
                m_new[mp] = jnp.maximum(m_old, mt_ref[par, mp])
                alpha_ref[par, mp] = jnp.exp2(m_old - m_new[mp])
                m_ref[mp] = m_new[mp]

        def piece(c):
            rows = slice(c * piece_rows, (c + 1) * piece_rows)
            for mp in range(2):
                s = s_ref[par, mp, rows, :].reshape(piece_rows // F32_SUBLANES, F32_SUBLANES, tq)
                p_ref[par, mp, rows, :] = jnp.exp2(s - m_new[mp][None]).reshape(piece_rows, tq).astype(BF16)

        return [start] + [functools.partial(piece, c) for c in range(n_chunks)]

    def weighted_values(t, par):
        def whole():
            off = pl.multiple_of(t * tk, tk)
            vt_tile = vt_all[:, pl.ds(off, tk)]
            for mp in range(2):
                acc = acc_ref[mp].reshape(-1, F32_SUBLANES, tq) * alpha_ref[par, mp][None]
                acc_ref[mp] = acc.reshape(acc_ref.shape[1:]) + _dot(vt_tile, p_ref[par, mp])
        return [whole]

    def run(*stages):
        for k in range(max(len(s) for s in stages)):
            for s in stages:
                if k < len(s):
                    s[k]()

    def stage_set(t, par):
        run(scores(t + 2, par), softmax(1 - par), weighted_values(t, par))

    run(scores(0, 0))
    if n_tiles > 1:
        run(scores(1, 1), softmax(0))
    else:
        run(softmax(0))
    n_sets = max(n_tiles - 2, 0)

    if unroll_sets:
        for t in range(n_sets):
            stage_set(t, t % 2)
    else:
        def body(i, _):
            stage_set(2 * i, 0)
            stage_set(2 * i + 1, 1)
            return 0

        lax.fori_loop(0, n_sets // 2, body, 0)
        if n_sets % 2 == 1:
            stage_set(n_sets - 1, (n_sets - 1) % 2)
    if n_tiles > 1:
        run(weighted_values(n_tiles - 2, (n_tiles - 2) % 2), softmax((n_tiles - 1) % 2))
    run(weighted_values(n_tiles - 1, (n_tiles - 1) % 2))

    lv = lam_ref[...]
    lam = (jnp.exp(jnp.sum(lv[0:1] * lv[1:2], axis=-1, keepdims=True))
           - jnp.exp(jnp.sum(lv[2:3] * lv[3:4], axis=-1, keepdims=True)) + lambda_init)
    a1, a2 = acc_ref[0], acc_ref[1]
    o = (a1[:ATT_VD] * (1.0 / a1[ATT_VD:ATT_VD + 1])
         - lam * (a2[:ATT_VD] * (1.0 / a2[ATT_VD:ATT_VD + 1])))
    o = o * lax.rsqrt(jnp.mean(o * o, axis=0, keepdims=True) + SUBLN_EPS)
    o_ref[...] = (o.T * g_ref[...] * (1.0 - lambda_init)).astype(o_ref.dtype)


def _key_tile(n_keys, cap):
    best = 256
    for t in range(256, cap + 1, 256):
        if n_keys % t == 0:
            best = t
    return best


def _diff_attention(lam_rows, subln_g, q_src, kv_srcs, *, batch, q_len, key_lens, lambda_init, v_block):
    h = ATT_HEADS
    tq = min(256, q_len)
    n_keys = sum(key_lens)
    tk = _key_tile(n_keys, 1536)
    nq = q_len // tq
    v_rows = ATT_VD + BF16_SUBLANES
    in_specs = [
        pl.BlockSpec((F32_SUBLANES, LANES), lambda b, hd, i: (0, 0)),
        pl.BlockSpec((1, ATT_VD), lambda b, hd, i: (0, 0)),
        pl.BlockSpec((tq, LANES), lambda b, hd, i: (b * nq + i, hd)),
    ]
    args = [lam_rows, subln_g.reshape(1, ATT_VD), q_src]
    for (k_src, v_src), klen in zip(kv_srcs, key_lens):
        in_specs += [pl.BlockSpec((klen, LANES), lambda b, hd, i: (b, h + hd)),
                     pl.BlockSpec((klen, LANES), lambda b, hd, i: (b, v_block + hd))]
        args += [k_src, v_src]
    scratch = [pltpu.VMEM((n_keys, LANES), BF16), pltpu.VMEM((v_rows, n_keys), BF16),
               pltpu.VMEM((2, LANES, tq), BF16), pltpu.VMEM((2, 2, tk, tq), F32),
               pltpu.VMEM((2, 2, tk, tq), BF16), pltpu.VMEM((2, 2, F32_SUBLANES, tq), F32),
               pltpu.VMEM((2, 2, F32_SUBLANES, tq), F32), pltpu.VMEM((2, F32_SUBLANES, tq), F32), pltpu.VMEM((2, v_rows, tq), F32)]
    return pl.pallas_call(
        functools.partial(_attn_kernel, key_lens=tuple(key_lens), tk=tk, lambda_init=lambda_init,
                          piece_rows=tk, unroll_sets=True),
        grid=(batch, h, nq),
        in_specs=in_specs,
        out_specs=pl.BlockSpec((tq, ATT_VD), lambda b, hd, i: (b * nq + i, hd)),
        out_shape=jax.ShapeDtypeStruct((batch * q_len, h * ATT_VD), BF16),
        scratch_shapes=scratch,
        compiler_params=_cparams(3),
        name="diff_attention",
    )(*args)


def _gelu(x):
    return 0.5 * x * (1.0 + lax.erf(x * (2.0 ** -0.5)))


def _sgu_kernel(su_ref, sv_ref, w_ref, b_ref, g_ref, o_ref):
    for c in range(su_ref.shape[0] // SGU_CHUNK):
        rows = slice(c * SGU_CHUNK, (c + 1) * SGU_CHUNK)
        u = _gelu(su_ref[rows, :].astype(F32))
        v = _gelu(sv_ref[rows, :].astype(F32))
        v = (v * lax.rsqrt(jnp.mean(v * v, axis=-1, keepdims=True) + EPS) * g_ref[...]).astype(BF16)
        for g in range(SGU_GROUPS):
            cols = slice(g * SGU_CH, (g + 1) * SGU_CH)
            mixed = _dot(w_ref[g], v[:, cols]) + b_ref[g]
            o_ref[rows, cols] = (u[:, cols] * mixed).astype(o_ref.dtype)


def _spatial_gating(p, w_s, b_s_lanes, g_v, *, su_block, rows_per_batch):
    m = p.shape[0]
    width = SGU_GROUPS * SGU_CH
    tm = min(512, rows_per_batch)
    return pl.pallas_call(
        _sgu_kernel,
        grid=(m // tm,),
        in_specs=[
            pl.BlockSpec((tm, width), lambda i: (i, su_block)),
            pl.BlockSpec((tm, width), lambda i: (i, su_block + 1)),
            pl.BlockSpec(w_s.shape, lambda i: (0, 0, 0)),
            pl.BlockSpec(b_s_lanes.shape, lambda i: (0, 0, 0)),
            pl.BlockSpec((1, width), lambda i: (0, 0)),
        ],
        out_specs=pl.BlockSpec((tm, width), lambda i: (i, 0)),
        out_shape=jax.ShapeDtypeStruct((m, width), BF16),
        compiler_params=_cparams(1),
        name="spatial_gating",
    )(p, p, w_s, b_s_lanes, g_v.reshape(1, width))


def _dft_cos_sin(n, scale):
    j = jnp.arange(n, dtype=jnp.int32)
    ang = ((j[:, None] * j[None, :]) % n).astype(F32) * (2.0 * math.pi / n)
    return jnp.cos(ang) * scale, jnp.sin(ang) * scale


def _fourier_stage1_kernel(x_ref, w_ref, tc_ref, ts_ref, o_ref):
    n2 = x_ref.shape[0]
    y = _dot(w_ref[...], x_ref[...])
    br, bi = y[:n2], y[n2:]
    reps = FOURIER_CH // LANES

    def widen(t):
        parts = [t[:, q * LANES:(q + 1) * LANES] for q in range(t.shape[1] // LANES)]
        return jnp.concatenate([p for p in parts for _ in range(reps)], axis=1)

    tc, ts = widen(tc_ref[...]), widen(ts_ref[...])
    o_ref[0] = (br * tc + bi * ts).astype(o_ref.dtype)
    o_ref[1] = (bi * tc - br * ts).astype(o_ref.dtype)


def _fourier_stage2_kernel(b_ref, w3_ref, cc_ref, sc_ref, o_ref):
    _, r, n1, ch = b_ref.shape
    w3, cc, sc = w3_ref[...], cc_ref[...], sc_ref[...]
    for j in range(r):
        bj = jnp.concatenate([b_ref[0, j], b_ref[1, j]], axis=0)
        y = _dot(w3, bj)
        out = _dot(y[:n1].astype(BF16), cc) + _dot(y[n1:].astype(BF16), sc)
        o_ref[:, j, :] = out


def _fourier_long(pf, *, batch, seq):
    n1 = LANES
    n2 = seq // n1
    ch = FOURIER_CH
    g_n = FOURIER_GROUPS
    r1 = min(16, n1)
    r2 = F32_SUBLANES

    c2, s2 = _dft_cos_sin(n2, n2 ** -0.5)
    w1 = jnp.concatenate([c2, -s2], axis=0).astype(BF16)

    kk = jnp.arange(n2, dtype=jnp.int32)[:, None] * jnp.arange(n1, dtype=jnp.int32)[None, :]
    ang = (kk % seq).astype(F32) * (2.0 * math.pi / seq)
    tw_c = jnp.broadcast_to(jnp.cos(ang)[:, :, None], (n2, n1, LANES)).reshape(n2, n1 * LANES)
    tw_s = jnp.broadcast_to(jnp.sin(ang)[:, :, None], (n2, n1, LANES)).reshape(n2, n1 * LANES)

    c1, s1 = _dft_cos_sin(n1, n1 ** -0.5)
    k3 = jnp.concatenate([jnp.concatenate([c1, s1], axis=1),
                          jnp.concatenate([-s1, c1], axis=1)], axis=0).astype(BF16)
    cc, sc = _dft_cos_sin(ch, ch ** -0.5)
    cc, sc = cc.astype(BF16), sc.astype(BF16)

    x2 = pf.reshape(batch, g_n, n2, n1 * ch)
    stage1 = pl.pallas_call(
        _fourier_stage1_kernel,
        grid=(batch, g_n, n1 // r1),
        in_specs=[
            pl.BlockSpec((None, None, n2, r1 * ch), lambda b, g, i: (b, g, 0, i)),
            _resident(w1.shape, lambda b, g, i: (0, 0)),
            pl.BlockSpec((n2, r1 * LANES), lambda b, g, i: (0, i)),
            pl.BlockSpec((n2, r1 * LANES), lambda b, g, i: (0, i)),
        ],
        out_specs=pl.BlockSpec((None, None, 2, n2, r1 * ch), lambda b, g, i: (b, g, 0, 0, i)),
        out_shape=jax.ShapeDtypeStruct((batch, g_n, 2, n2, n1 * ch), BF16),
        compiler_params=_cparams(3),
        name="fourier_stage1",
    )(x2, w1, tw_c, tw_s)
    stage1 = stage1.reshape(batch, g_n, 2, n2, n1, ch)

    out = pl.pallas_call(
        _fourier_stage2_kernel,
        grid=(batch, g_n, n2 // r2),
        in_specs=[
            pl.BlockSpec((None, None, 2, r2, n1, ch), lambda b, g, i: (b, g, 0, i, 0, 0)),
            _resident(k3.shape, lambda b, g, i: (0, 0)),
            _resident(cc.shape, lambda b, g, i: (0, 0)),
            _resident(sc.shape, lambda b, g, i: (0, 0)),
        ],
        out_specs=pl.BlockSpec((None, n1, r2, ch), lambda b, g, i: (b, 0, i, g)),
        out_shape=jax.ShapeDtypeStruct((batch, n1, n2, g_n * ch), F32),
        compiler_params=_cparams(3),
        name="fourier_stage2",
    )(stage1, k3, cc, sc)
    return out.reshape(batch * seq, g_n * ch)


def _fourier_dense_kernel(x_ref, cl_ref, sl_ref, cc_ref, sc_ref, o_ref):
    x = x_ref[...]
    a = _dot(cl_ref[...], x).astype(BF16)
    b = _dot(sl_ref[...], x).astype(BF16)
    o_ref[...] = _dot(a, cc_ref[...]) - _dot(b, sc_ref[...])


def _fourier_short(pf, *, batch, seq):
    ch = FOURIER_CH
    g_n = FOURIER_GROUPS
    cl, sl = _dft_cos_sin(seq, seq ** -0.5)
    cc, sc = _dft_cos_sin(ch, ch ** -0.5)
    const = lambda a: _resident(a.shape, lambda b, g: (0, 0))
    mats = [m.astype(BF16) for m in (cl, sl, cc, sc)]
    return pl.pallas_call(
        _fourier_dense_kernel,
        grid=(batch, g_n),
        in_specs=[pl.BlockSpec((None, None, seq, ch), lambda b, g: (b, g, 0, 0))] + [const(m) for m in mats],
        out_specs=pl.BlockSpec((seq, ch), lambda b, g: (b, g)),
        out_shape=jax.ShapeDtypeStruct((batch * seq, g_n * ch), F32),
        compiler_params=_cparams(2),
        name="fourier_dense",
    )(pf, *mats)


def _merge_kernel(att_ref, sgu_ref, four_ref, ga_ref, gg_ref, gf_ref, wa_ref, ws_ref, wf_ref, wo_ref,
                  x_ref, gate_ref, g_ref, sh_ref, sc_ref, xo_ref, ho_ref):
    y = jax.nn.sigmoid(ga_ref[...].astype(F32)) * _dot(att_ref[...], wa_ref[...])
    y = y + jax.nn.sigmoid(gg_ref[...].astype(F32)) * _dot(sgu_ref[...], ws_ref[...])
    y = y + jax.nn.sigmoid(gf_ref[...].astype(F32)) * _dot(four_ref[...].astype(BF16), wf_ref[...])
    x = x_ref[...] + gate_ref[...] * _dot(y.astype(BF16), wo_ref[...])
    xo_ref[...] = x
    h = x * lax.rsqrt(jnp.mean(x * x, axis=-1, keepdims=True) + EPS) * g_ref[...]
    ho_ref[...] = (h * (1.0 + sc_ref[...]) + sh_ref[...]).astype(ho_ref.dtype)


def _merge_project_norm(att, sgu, four, p, w_pa, w_ps, w_pf, w_o, x2d, gate, g, shift, scale, *,
                        gate_block, rows_per_batch):
    m, k = att.shape
    d = w_pa.shape[1]
    tm = min(256, rows_per_batch)
    tpb = rows_per_batch // tm
    act = pl.BlockSpec((tm, k), lambda i: (i, 0))
    gcol = lambda off: pl.BlockSpec((tm, d), lambda i: (i, gate_block + off))
    row = pl.BlockSpec((tm, d), lambda i: (i, 0))
    wspec = _resident((k, d), lambda i: (0, 0))
    vec = pl.BlockSpec((None, 1, d), lambda i: (i // tpb, 0, 0))
    return pl.pallas_call(
        _merge_kernel,
        grid=(m // tm,),
        in_specs=[act, act, act, gcol(0), gcol(1), gcol(2), wspec, wspec, wspec,
                  _resident((d, d), lambda i: (0, 0)), row, vec,
                  pl.BlockSpec((1, d), lambda i: (0, 0)), vec, vec],
        out_specs=[row, row],
        out_shape=[jax.ShapeDtypeStruct((m, d), F32), jax.ShapeDtypeStruct((m, d), BF16)],
        compiler_params=_cparams(1),
        name="merge_outproj_norm",
    )(att, sgu, four, p, p, p, w_pa, w_ps, w_pf, w_o, x2d, gate, g.reshape(1, d), shift, scale)


def _rope_tables(n_tokens):
    n_rows = n_tokens // GRID_W
    row = jnp.broadcast_to(jnp.arange(n_rows, dtype=F32)[:, None], (n_rows, GRID_W)).reshape(-1)
    col = jnp.broadcast_to(jnp.arange(GRID_W, dtype=F32)[None, :], (n_rows, GRID_W)).reshape(-1)
    n_freq = ATT_HD // 4
    inv = ROPE_BASE ** (-jnp.arange(n_freq, dtype=F32) / n_freq)
    ar = row[:, None] * inv
    ac = col[:, None] * inv
    ang = jnp.concatenate([ar, ar, ac, ac] * (LANES // ATT_HD), axis=-1)
    cos, sin = jnp.cos(ang), jnp.sin(ang)
    first_half = (jnp.arange(LANES) % (ATT_HD // 2)) < (ATT_HD // 4)
    return cos, jnp.where(first_half, -sin, 0.0), jnp.where(first_half, 0.0, sin)


def _split_in_projection(w_in):
    depth, d, _ = w_in.shape
    qk = ATT_HEADS * ATT_HD
    v0 = 4 * qk
    su0 = v0 + ATT_HEADS * ATT_VD
    f0 = su0 + 2 * SGU_GROUPS * SGU_CH
    g0 = f0 + FOURIER_GROUPS * FOURIER_CH
    seg = lambda s: w_in[:, :, s * qk:(s + 1) * qk].reshape(depth, d, ATT_HEADS, ATT_HD)
    pair = lambda a, b: jnp.concatenate([seg(a), seg(b)], axis=-1).reshape(depth, d, 2 * qk)
    w_qk = jnp.concatenate([pair(0, 1), pair(2, 3)], axis=-1)
    w_rest = jnp.concatenate([w_in[:, :, g0:], w_in[:, :, v0:f0]], axis=-1)
    return w_qk.astype(BF16), w_rest.astype(BF16), w_in[:, :, f0:g0].astype(BF16)


def kernel(x, c, ctx, c_ctx, w_mod, b_mod, norm1_g, norm2_g, w_in, lambda_q1, lambda_k1, lambda_q2, lambda_k2,
           subln_g, sgu_norm_g, sgu_w, sgu_b, w_proj_att, w_proj_sgu, w_proj_fourier, w_out, w_mlp_in, w_mlp_out,
           final_g):
    batch, seq, d = x.shape
    ctx_len = ctx.shape[1]
    depth = w_mod.shape[0]
    in_width = w_in.shape[2]
    qk_width = 2 * ATT_HEADS * ATT_HD
    v_width = ATT_HEADS * ATT_VD
    sgu_width = SGU_GROUPS * SGU_CH
    four_width = FOURIER_GROUPS * FOURIER_CH
    assert in_width == 2 * qk_width + v_width + 2 * sgu_width + four_width + 3 * d
    assert seq % (LANES * F32_SUBLANES) == 0 and seq % GRID_W == 0 and ctx_len % 256 == 0
    gate_block = 0
    v_block = 3 * d // LANES
    su_block = (3 * d + v_width) // sgu_width
    assert (3 * d + v_width) % sgu_width == 0

    w_qk, w_rest, w_four = _split_in_projection(w_in)
    w_pa, w_ps, w_pf = (w.astype(BF16) for w in (w_proj_att, w_proj_sgu, w_proj_fourier))
    w_o, w_1, w_2 = (w.astype(BF16) for w in (w_out, w_mlp_in, w_mlp_out))
    sgu_w_b = sgu_w.astype(BF16)
    sgu_b_lanes = jnp.broadcast_to(sgu_b[..., None], sgu_b.shape + (SGU_CH,))

    cos, sin_up, sin_down = _rope_tables(seq)
    ctx_rows = min(1024, ctx_len)
    one_t, zero_t = jnp.ones((ctx_rows, LANES), F32), jnp.zeros((ctx_rows, LANES), F32)

    pad = jnp.zeros((F32_SUBLANES - batch - 1, d), F32)
    c_rows = jnp.concatenate([c, c_ctx[None, :], pad], axis=0)
    mod = _modulation(c_rows, w_mod, b_mod)

    xs =x.reshape(batch * seq, d)
    cs = ctx.reshape(batch * ctx_len, d)
    for l in range(depth):
        last = l == depth - 1
        lambda_init = 0.8 - 0.6 * math.exp(-0.3 * l)
        lam_rows = jnp.stack([lambda_q1[l], lambda_k1[l], lambda_q2[l], lambda_k2[l]]).astype(F32)
        lam_rows = jnp.pad(lam_rows, ((0, F32_SUBLANES - 4), (0, LANES - ATT_HD)))
        mod_x = mod[l, :batch].reshape(batch, 1, N_MOD, d)
        mod_c = mod[l, batch:batch + 1].reshape(1, 1, N_MOD, d)
        mx = [mod_x[:, :, i] for i in range(N_MOD)]
        mc = [mod_c[:, :, i] for i in range(N_MOD)]

        hx = _rmsnorm(xs, norm1_g[l], mx[0], mx[1], rows_per_batch=seq, out_dtype=BF16)
        hc = _rmsnorm(cs, norm1_g[l], mc[0], mc[1], rows_per_batch=batch * ctx_len, out_dtype=BF16)
        qk_x = _in_projection_qk(hx, w_qk[l], cos, sin_up, sin_down, rows_per_batch=seq)
        qk_c = _in_projection_qk(hc, w_qk[l], one_t, zero_t, zero_t, rows_per_batch=ctx_len)
        px = _in_projection_plain(hx, w_rest[l])
        pc = _in_projection_plain(hc, w_rest[l])
        fx = _in_projection_fourier(hx, w_four[l], batch=batch)

        att_x = _diff_attention(lam_rows, subln_g[l], qk_x, [(qk_x, px), (qk_c, pc)], batch=batch, q_len=seq,
                                key_lens=[seq, ctx_len], lambda_init=lambda_init, v_block=v_block)
        sgu_x = _spatial_gating(px, sgu_w_b[l], sgu_b_lanes[l], sgu_norm_g[l], su_block=su_block,
                                rows_per_batch=seq)
        four_x = _fourier_long(fx, batch=batch, seq=seq)

        if not last:
            fc = _in_projection_fourier(hc, w_four[l], batch=batch)
            att_c = _diff_attention(lam_rows, subln_g[l], qk_c, [(qk_c, pc)], batch=batch, q_len=ctx_len,
                                    key_lens=[ctx_len], lambda_init=lambda_init, v_block=v_block)
            sgu_c = _spatial_gating(pc, sgu_w_b[l], sgu_b_lanes[l], sgu_norm_g[l], su_block=su_block,
                                    rows_per_batch=ctx_len)
            four_c = _fourier_short(fc, batch=batch, seq=ctx_len)
            cs, hc2 = _merge_project_norm(att_c, sgu_c, four_c, pc, w_pa[l], w_ps[l], w_pf[l], w_o[l], cs,
                                          mc[2], norm2_g[l], mc[3], mc[4], gate_block=gate_block,
                                          rows_per_batch=batch * ctx_len)
            cs = _matmul_gated_residual(_matmul_relu2(hc2, w_1[l]), w_2[l], cs, mc[5],
                                        rows_per_batch=batch * ctx_len, name="mlp_out_residual")

        xs, hx2 = _merge_project_norm(att_x, sgu_x, four_x, px, w_pa[l], w_ps[l], w_pf[l], w_o[l], xs,
                                      mx[2], norm2_g[l], mx[3], mx[4], gate_block=gate_block,
                                      rows_per_batch=seq)
        xs = _matmul_gated_residual(_matmul_relu2(hx2, w_1[l]), w_2[l], xs, mx[5], rows_per_batch=seq,
                                    name="mlp_out_residual")

    out = _rmsnorm(xs, final_g, rows_per_batch=seq, out_dtype=x.dtype)
    return out.reshape(batch, seq, d)
```

```python
import functools
import math

import jax
import jax.numpy as jnp
from jax import lax
from jax.experimental import pallas as pl
from jax.experimental.pallas import tpu as pltpu

F32 = jnp.float32
BF16 = jnp.bfloat16

GRID_W = 64
ATT_HEADS = 8
ATT_HD = 64
ATT_VD = 2 * ATT_HD
ROPE_BASE = 10000.0
SGU_GROUPS = 8
SGU_CH = 128
SGU_CHUNK = 128
FOURIER_GROUPS = 4
FOURIER_CH = 256
N_MOD = 6
EPS = 1e-6
SUBLN_EPS = 1e-5

LANES = 128
BF16_SUBLANES = 16
F32_SUBLANES = 8
VMEM_LIMIT_BYTES = 56 * 2**20

LOG2E = 1.4426950408889634
Q_PRESCALE = (ATT_HD ** -0.5) * LOG2E
NEG_BIG = -1e30
MAX_SLAB = 64


def _cparams(n_axes):
    return pltpu.CompilerParams(dimension_semantics=("arbitrary",) * n_axes,
                                vmem_limit_bytes=VMEM_LIMIT_BYTES)


def _dot(a, b):
    return jnp.dot(a, b, preferred_element_type=F32)


def _resident(shape, index_map):
    return pl.BlockSpec(shape, index_map, pipeline_mode=pl.Buffered(1))


def _mod_kernel(c_ref, w_ref, b_ref, o_ref):
    a = c_ref[...]
    a = a * jax.nn.sigmoid(a)
    w = w_ref[...]
    a_hi = a.astype(BF16)
    a_lo = (a - a_hi.astype(F32)).astype(BF16)
    w_hi = w.astype(BF16)
    w_lo = (w - w_hi.astype(F32)).astype(BF16)
    o_ref[...] = _dot(a_hi, w_hi) + _dot(a_hi, w_lo) + _dot(a_lo, w_hi) + b_ref[...]


def _modulation(c_rows, w_mod, b_mod):
    depth, d, n = w_mod.shape
    tn = 512
    rows = c_rows.shape[0]
    return pl.pallas_call(
        _mod_kernel,
        grid=(depth, n // tn),
        in_specs=[
            pl.BlockSpec((rows, d), lambda l, j: (0, 0)),
            pl.BlockSpec((None, d, tn), lambda l, j: (l, 0, j)),
            pl.BlockSpec((None, 1, tn), lambda l, j: (l, 0, j)),
        ],
        out_specs=pl.BlockSpec((None, rows, tn), lambda l, j: (l, 0, j)),
        out_shape=jax.ShapeDtypeStruct((depth, rows, n), F32),
        compiler_params=_cparams(2),
        name="adaln_mod",
    )(c_rows, w_mod, b_mod.reshape(depth, 1, n))


def _norm_kernel(x_ref, g_ref, *rest, eps, modulated):
    x = x_ref[...]
    y = x * lax.rsqrt(jnp.mean(x * x, axis=-1, keepdims=True) + eps) * g_ref[...]
    if modulated:
        sh_ref, sc_ref, o_ref = rest
        y = y * (1.0 + sc_ref[...]) + sh_ref[...]
    else:
        (o_ref,) = rest
    o_ref[...] = y.astype(o_ref.dtype)


def _rmsnorm(x2d, g, shift=None, scale=None, *, rows_per_batch, out_dtype):
    m, d = x2d.shape
    tm = min(512, rows_per_batch)
    tpb = rows_per_batch // tm
    modulated = shift is not None
    in_specs = [pl.BlockSpec((tm, d), lambda i: (i, 0)), pl.BlockSpec((1, d), lambda i: (0, 0))]
    args = [x2d, g.reshape(1, d)]
    if modulated:
        vec = pl.BlockSpec((None, 1, d), lambda i: (i // tpb, 0, 0))
        in_specs += [vec, vec]
        args += [shift, scale]
    return pl.pallas_call(
        functools.partial(_norm_kernel, eps=EPS, modulated=modulated),
        grid=(m // tm,),
        in_specs=in_specs,
        out_specs=pl.BlockSpec((tm, d), lambda i: (i, 0)),
        out_shape=jax.ShapeDtypeStruct((m, d), out_dtype),
        compiler_params=_cparams(1),
        name="rmsnorm_mod" if modulated else "rmsnorm",
    )(*args)


def _mm_rope_kernel(a_ref, w_ref, cos_ref, sa_ref, sb_ref, o_ref, *, n_q_tiles):
    acc = _dot(a_ref[...], w_ref[...])
    cos, sa, sb = cos_ref[...], sa_ref[...], sb_ref[...]
    scale = jnp.where(pl.program_id(1) < n_q_tiles, Q_PRESCALE, 1.0).astype(F32)
    for c in range(acc.shape[1] // LANES):
        xc = acc[:, c * LANES:(c + 1) * LANES]
        up = pltpu.roll(xc, LANES - 16, 1)
        down = pltpu.roll(xc, 16, 1)
        r = xc * cos + up * sa + down * sb
        o_ref[:, c * LANES:(c + 1) * LANES] = (r * scale).astype(o_ref.dtype)


def _in_projection_qk(h, w, cos, sin_up, sin_down, *, rows_per_batch):
    m, k = h.shape
    n = w.shape[1]
    tm = min(1024, rows_per_batch)
    tn = 1024
    tpb = rows_per_batch // tm
    table = pl.BlockSpec((tm, LANES), lambda i, j: (i % tpb, 0))
    return pl.pallas_call(
        functools.partial(_mm_rope_kernel, n_q_tiles=n // (2 * tn)),
        grid=(m // tm, n // tn),
        in_specs=[
            pl.BlockSpec((tm, k), lambda i, j: (i, 0)),
            pl.BlockSpec((k, tn), lambda i, j: (0, j)),
            table, table, table,
        ],
        out_specs=pl.BlockSpec((tm, tn), lambda i, j: (i, j)),
        out_shape=jax.ShapeDtypeStruct((m, n), BF16),
        compiler_params=_cparams(2),
        name="in_proj_qk",
    )(h, w, cos, sin_up, sin_down)


def _mm_kernel(a_ref, w_ref, o_ref):
    o_ref[...] = _dot(a_ref[...], w_ref[...]).astype(o_ref.dtype)


def _in_projection_plain(h, w):
    m, k = h.shape
    n = w.shape[1]
    tm = min(1024, m)
    tn = 1024
    return pl.pallas_call(
        _mm_kernel,
        grid=(m // tm, n // tn),
        in_specs=[pl.BlockSpec((tm, k), lambda i, j: (i, 0)), pl.BlockSpec((k, tn), lambda i, j: (0, j))],
        out_specs=pl.BlockSpec((tm, tn), lambda i, j: (i, j)),
        out_shape=jax.ShapeDtypeStruct((m, n), BF16),
        compiler_params=_cparams(2),
        name="in_proj_plain",
    )(h, w)


def _in_projection_fourier(h, w, *, batch):
    m, k = h.shape
    seq = m // batch
    tm = min(1024, seq)
    tpb = seq // tm
    return pl.pallas_call(
        _mm_kernel,
        grid=(m // tm, FOURIER_GROUPS),
        in_specs=[pl.BlockSpec((tm, k), lambda i, g: (i, 0)),
                  pl.BlockSpec((k, FOURIER_CH), lambda i, g: (0, g))],
        out_specs=pl.BlockSpec((None, None, tm, FOURIER_CH), lambda i, g: (i // tpb, g, i % tpb, 0)),
        out_shape=jax.ShapeDtypeStruct((batch, FOURIER_GROUPS, seq, FOURIER_CH), BF16),
        compiler_params=_cparams(2),
        name="in_proj_fourier",
    )(h, w)


def _mm_relu2_kernel(a_ref, w_ref, o_ref):
    acc = jnp.maximum(_dot(a_ref[...], w_ref[...]), 0.0)
    o_ref[...] = (acc * acc).astype(o_ref.dtype)


def _matmul_relu2(a, w):
    m, k = a.shape
    n = w.shape[1]
    tm = min(1024, m)
    tn = 1024
    return pl.pallas_call(
        _mm_relu2_kernel,
        grid=(m // tm, n // tn),
        in_specs=[pl.BlockSpec((tm, k), lambda i, j: (i, 0)), pl.BlockSpec((k, tn), lambda i, j: (0, j))],
        out_specs=pl.BlockSpec((tm, tn), lambda i, j: (i, j)),
        out_shape=jax.ShapeDtypeStruct((m, n), BF16),
        compiler_params=_cparams(2),
        name="mlp_in_relu2",
    )(a, w)


def _mm_resid_kernel(a_ref, w_ref, x_ref, gate_ref, o_ref):
    o_ref[...] = x_ref[...] + gate_ref[...] * _dot(a_ref[...], w_ref[...])


def _matmul_gated_residual(a, w, x2d, gate, *, rows_per_batch, name):
    m, k = a.shape
    n = w.shape[1]
    tm = min(1024, rows_per_batch)
    tn = 256
    tpb = rows_per_batch // tm
    return pl.pallas_call(
        _mm_resid_kernel,
        grid=(m // tm, n // tn),
        in_specs=[
            pl.BlockSpec((tm, k), lambda i, j: (i, 0)),
            pl.BlockSpec((k, tn), lambda i, j: (0, j)),
            pl.BlockSpec((tm, tn), lambda i, j: (i, j)),
            pl.BlockSpec((None, 1, tn), lambda i, j: (i // tpb, 0, j)),
        ],
        out_specs=pl.BlockSpec((tm, tn), lambda i, j: (i, j)),
        out_shape=jax.ShapeDtypeStruct((m, n), F32),
        compiler_params=_cparams(2),
        name=name,
    )(a, w, x2d, gate)


def _attn_kernel(lam_ref, g_ref, q_ref, *refs, key_lens, tk, tq, lambda_init):
    n_src = len(key_lens)
    k_refs = refs[0:2 * n_src:2]
    v_refs = refs[1:2 * n_src:2]
    o_ref = refs[2 * n_src]
    k_all, vt_all, rhs_ref, s_ref, p_ref, mt_ref, alpha_ref, m_ref, acc_ref = refs[2 * n_src + 1:]
    q_blocks = q_ref.shape[0] // tq
    n_keys = sum(key_lens)
    n_tiles = n_keys // tk
    chunk = 256

    @pl.when(pl.program_id(2) == 0)
    def _():
        base = 0
        for k_ref, v_ref, klen in zip(k_refs, v_refs, key_lens):
            def cp(c, _, k_ref=k_ref, v_ref=v_ref, base=base):
                off = pl.multiple_of(c * chunk, chunk)
                k_all[pl.ds(base + off, chunk), :] = k_ref[pl.ds(off, chunk), :]
                vt_all[0:ATT_VD, pl.ds(base + off, chunk)] = (
                    v_ref[pl.ds(off, chunk), :].astype(F32).T.astype(BF16))
                return 0
            lax.fori_loop(0, klen // chunk, cp, 0)
            base += klen
        vt_all[ATT_VD:, :] = jnp.ones((vt_all.shape[0] - ATT_VD, n_keys), BF16)

    for u in range(q_blocks):
        qt = q_ref[u * tq:(u + 1) * tq, :].astype(F32).T
        row = lax.broadcasted_iota(jnp.int32, qt.shape, 0)
        rhs_ref[u, 0] = jnp.where(row < ATT_HD, qt, 0.0).astype(BF16)
        rhs_ref[u, 1] = jnp.where(row >= ATT_HD, qt, 0.0).astype(BF16)
    m_ref[...] = jnp.full(m_ref.shape, NEG_BIG, F32)
    stat_shape = m_ref.shape[2:]
    acc_ref[...] = jnp.zeros(acc_ref.shape, F32)

    def scores(g):
        u, t = divmod(g, n_tiles)
        par = g % 2
        k_tile = k_all[t * tk:(t + 1) * tk, :]
        for mp in range(2):
            s = _dot(k_tile, rhs_ref[u, mp])
            s_ref[par, mp] = s
            slab = jnp.max(s.reshape(tk // MAX_SLAB, MAX_SLAB, tq), axis=0)
            mt_ref[par, mp] = jnp.broadcast_to(jnp.max(slab, axis=0, keepdims=True), stat_shape)

    def softmax(g):
        u = g // n_tiles
        par = g % 2
        for mp in range(2):
            m_old = m_ref[u, mp]
            m_new = jnp.maximum(m_old, mt_ref[par, mp])
            alpha_ref[par, mp] = jnp.exp2(m_old - m_new)
            m_ref[u, mp] = m_new
            s = s_ref[par, mp].reshape(tk // F32_SUBLANES, F32_SUBLANES, tq)
            p_ref[par, mp] = jnp.exp2(s - m_new[None]).reshape(tk, tq).astype(BF16)

    def weighted_values(g):
        u, t = divmod(g, n_tiles)
        par = g % 2
        vt_tile = vt_all[:, t * tk:(t + 1) * tk]
        for mp in range(2):
            acc = acc_ref[u, mp].reshape(-1, F32_SUBLANES, tq) * alpha_ref[par, mp][None]
            acc_ref[u, mp] = acc.reshape(acc_ref.shape[2:]) + _dot(vt_tile, p_ref[par, mp])
        if t == n_tiles - 1:
            finalize(u)

    def finalize(u):
        lv = lam_ref[...]
        lam = (jnp.exp(jnp.sum(lv[0:1] * lv[1:2], axis=-1, keepdims=True))
               - jnp.exp(jnp.sum(lv[2:3] * lv[3:4], axis=-1, keepdims=True)) + lambda_init)
        a1, a2 = acc_ref[u, 0], acc_ref[u, 1]
        o = (a1[:ATT_VD] * (1.0 / a1[ATT_VD:ATT_VD + 1])
             - lam * (a2[:ATT_VD] * (1.0 / a2[ATT_VD:ATT_VD + 1])))
        o = o * lax.rsqrt(jnp.mean(o * o, axis=0, keepdims=True) + SUBLN_EPS)
        o_ref[u * tq:(u + 1) * tq, :] = (o.T * g_ref[...] * (1.0 - lambda_init)).astype(o_ref.dtype)

    total = q_blocks * n_tiles
    scores(0)
    if total > 1:
        scores(1)
    softmax(0)
    for g in range(total):
        if g + 2 < total:
            scores(g + 2)
        if g + 1 < total:
            softmax(g + 1)
        weighted_values(g)


def _key_tile(n_keys, cap):
    best = 256
    for t in range(256, cap + 1, 256):
        if n_keys % t == 0:
            best = t
    return best


def _diff_attention(lam_rows, subln_g, q_src, kv_srcs, *, batch, q_len, key_lens, lambda_init, v_block):
    h = ATT_HEADS
    tq = min(256, q_len)
    q_blocks = 2 if (q_len // tq) % 2 == 0 else 1
    n_keys = sum(key_lens)
    tk = _key_tile(n_keys, 1536)
    nq = q_len // (tq * q_blocks)
    v_rows = ATT_VD + BF16_SUBLANES
    in_specs = [
        pl.BlockSpec((F32_SUBLANES, LANES), lambda b, hd, i: (0, 0)),
        pl.BlockSpec((1, ATT_VD), lambda b, hd, i: (0, 0)),
        pl.BlockSpec((q_blocks * tq, LANES), lambda b, hd, i: (b * nq + i, hd)),
    ]
    args = [lam_rows, subln_g.reshape(1, ATT_VD), q_src]
    for (k_src, v_src), klen in zip(kv_srcs, key_lens):
        in_specs += [pl.BlockSpec((klen, LANES), lambda b, hd, i: (b, h + hd)),
                     pl.BlockSpec((klen, LANES), lambda b, hd, i: (b, v_block + hd))]
        args += [k_src, v_src]
    scratch = [pltpu.VMEM((n_keys, LANES), BF16), pltpu.VMEM((v_rows, n_keys), BF16),
               pltpu.VMEM((q_blocks, 2, LANES, tq), BF16), pltpu.VMEM((2, 2, tk, tq), F32),
               pltpu.VMEM((2, 2, tk, tq), BF16), pltpu.VMEM((2, 2, F32_SUBLANES, tq), F32),
               pltpu.VMEM((2, 2, F32_SUBLANES, tq), F32), pltpu.VMEM((q_blocks, 2, F32_SUBLANES, tq), F32),
               pltpu.VMEM((q_blocks, 2, v_rows, tq), F32)]
    return pl.pallas_call(
        functools.partial(_attn_kernel, key_lens=tuple(key_lens), tk=tk, tq=tq, lambda_init=lambda_init),
        grid=(batch, h, nq),
        in_specs=in_specs,
        out_specs=pl.BlockSpec((q_blocks * tq, ATT_VD), lambda b, hd, i: (b * nq + i, hd)),
        out_shape=jax.ShapeDtypeStruct((batch * q_len, h * ATT_VD), BF16),
        scratch_shapes=scratch,
        compiler_params=_cparams(3),
        name="diff_attention",
    )(*args)


def _gelu(x):
    return 0.5 * x * (1.0 + lax.erf(x * (2.0 ** -0.5)))


def _sgu_kernel(su_ref, sv_ref, w_ref, b_ref, g_ref, o_ref):
    for c in range(su_ref.shape[0] // SGU_CHUNK):
        rows = slice(c * SGU_CHUNK, (c + 1) * SGU_CHUNK)
        u = _gelu(su_ref[rows, :].astype(F32))
        v = _gelu(sv_ref[rows, :].astype(F32))
        v = (v * lax.rsqrt(jnp.mean(v * v, axis=-1, keepdims=True) + EPS) * g_ref[...]).astype(BF16)
        for g in range(SGU_GROUPS):
            cols = slice(g * SGU_CH, (g + 1) * SGU_CH)
            mixed = _dot(w_ref[g], v[:, cols]) + b_ref[g]
            o_ref[rows, cols] = (u[:, cols] * mixed).astype(o_ref.dtype)


def _spatial_gating(p, w_s, b_s_lanes, g_v, *, su_block, rows_per_batch):
    m = p.shape[0]
    width = SGU_GROUPS * SGU_CH
    tm = min(512, rows_per_batch)
    return pl.pallas_call(
        _sgu_kernel,
        grid=(m // tm,),
        in_specs=[
            pl.BlockSpec((tm, width), lambda i: (i, su_block)),
            pl.BlockSpec((tm, width), lambda i: (i, su_block + 1)),
            pl.BlockSpec(w_s.shape, lambda i: (0, 0, 0)),
            pl.BlockSpec(b_s_lanes.shape, lambda i: (0, 0, 0)),
            pl.BlockSpec((1, width), lambda i: (0, 0)),
        ],
        out_specs=pl.BlockSpec((tm, width), lambda i: (i, 0)),
        out_shape=jax.ShapeDtypeStruct((m, width), BF16),
        compiler_params=_cparams(1),
        name="spatial_gating",
    )(p, p, w_s, b_s_lanes, g_v.reshape(1, width))


def _dft_cos_sin(n, scale):
    j = jnp.arange(n, dtype=jnp.int32)
    ang = ((j[:, None] * j[None, :]) % n).astype(F32) * (2.0 * math.pi / n)
    return jnp.cos(ang) * scale, jnp.sin(ang) * scale


def _fourier_stage1_kernel(x_ref, w_ref, tc_ref, ts_ref, o_ref):
    n2 = x_ref.shape[0]
    y = _dot(w_ref[...], x_ref[...])
    br, bi = y[:n2], y[n2:]
    reps = FOURIER_CH // LANES

    def widen(t):
        parts = [t[:, q * LANES:(q + 1) * LANES] for q in range(t.shape[1] // LANES)]
        return jnp.concatenate([p for p in parts for _ in range(reps)], axis=1)

    tc, ts = widen(tc_ref[...]), widen(ts_ref[...])
    o_ref[0] = (br * tc + bi * ts).astype(o_ref.dtype)
    o_ref[1] = (bi * tc - br * ts).astype(o_ref.dtype)


def _fourier_stage2_kernel(b_ref, w3_ref, cc_ref, sc_ref, o_ref):
    _, r, n1, ch = b_ref.shape
    w3, cc, sc = w3_ref[...], cc_ref[...], sc_ref[...]
    for j in range(r):
        bj = jnp.concatenate([b_ref[0, j], b_ref[1, j]], axis=0)
        y = _dot(w3, bj)
        out = _dot(y[:n1].astype(BF16), cc) + _dot(y[n1:].astype(BF16), sc)
        o_ref[:, j, :] = out


def _fourier_long(pf, *, batch, seq):
    n1 = LANES
    n2 = seq // n1
    ch = FOURIER_CH
    g_n = FOURIER_GROUPS
    r1 = min(16, n1)
    r2 = F32_SUBLANES

    c2, s2 = _dft_cos_sin(n2, n2 ** -0.5)
    w1 = jnp.concatenate([c2, -s2], axis=0).astype(BF16)

    kk = jnp.arange(n2, dtype=jnp.int32)[:, None] * jnp.arange(n1, dtype=jnp.int32)[None, :]
    ang = (kk % seq).astype(F32) * (2.0 * math.pi / seq)
    tw_c = jnp.broadcast_to(jnp.cos(ang)[:, :, None], (n2, n1, LANES)).reshape(n2, n1 * LANES)
    tw_s = jnp.broadcast_to(jnp.sin(ang)[:, :, None], (n2, n1, LANES)).reshape(n2, n1 * LANES)

    c1, s1 = _dft_cos_sin(n1, n1 ** -0.5)
    k3 = jnp.concatenate([jnp.concatenate([c1, s1], axis=1),
                          jnp.concatenate([-s1, c1], axis=1)], axis=0).astype(BF16)
    cc, sc = _dft_cos_sin(ch, ch ** -0.5)
    cc, sc = cc.astype(BF16), sc.astype(BF16)

    x2 = pf.reshape(batch, g_n, n2, n1 * ch)
    stage1 = pl.pallas_call(
        _fourier_stage1_kernel,
        grid=(batch, g_n, n1 // r1),
        in_specs=[
            pl.BlockSpec((None, None, n2, r1 * ch), lambda b, g, i: (b, g, 0, i)),
            _resident(w1.shape, lambda b, g, i: (0, 0)),
            pl.BlockSpec((n2, r1 * LANES), lambda b, g, i: (0, i)),
            pl.BlockSpec((n2, r1 * LANES), lambda b, g, i: (0, i)),
        ],
        out_specs=pl.BlockSpec((None, None, 2, n2, r1 * ch), lambda b, g, i: (b, g, 0, 0, i)),
        out_shape=jax.ShapeDtypeStruct((batch, g_n, 2, n2, n1 * ch), BF16),
        compiler_params=_cparams(3),
        name="fourier_stage1",
    )(x2, w1, tw_c, tw_s)
    stage1 = stage1.reshape(batch, g_n, 2, n2, n1, ch)

    out = pl.pallas_call(
        _fourier_stage2_kernel,
        grid=(batch, g_n, n2 // r2),
        in_specs=[
            pl.BlockSpec((None, None, 2, r2, n1, ch), lambda b, g, i: (b, g, 0, i, 0, 0)),
            _resident(k3.shape, lambda b, g, i: (0, 0)),
            _resident(cc.shape, lambda b, g, i: (0, 0)),
            _resident(sc.shape, lambda b, g, i: (0, 0)),
        ],
        out_specs=pl.BlockSpec((None, n1, r2, ch), lambda b, g, i: (b, 0, i, g)),
        out_shape=jax.ShapeDtypeStruct((batch, n1, n2, g_n * ch), F32),
        compiler_params=_cparams(3),
        name="fourier_stage2",
    )(stage1, k3, cc, sc)
    return out.reshape(batch * seq, g_n * ch)


def _fourier_dense_kernel(x_ref, cl_ref, sl_ref, cc_ref, sc_ref, o_ref):
    x = x_ref[...]
    a = _dot(cl_ref[...], x).astype(BF16)
    b = _dot(sl_ref[...], x).astype(BF16)
    o_ref[...] = _dot(a, cc_ref[...]) - _dot(b, sc_ref[...])


def _fourier_short(pf, *, batch, seq):
    ch = FOURIER_CH
    g_n = FOURIER_GROUPS
    cl, sl = _dft_cos_sin(seq, seq ** -0.5)
    cc, sc = _dft_cos_sin(ch, ch ** -0.5)
    const = lambda a: _resident(a.shape, lambda b, g: (0, 0))
    mats = [m.astype(BF16) for m in (cl, sl, cc, sc)]
    return pl.pallas_call(
        _fourier_dense_kernel,
        grid=(batch, g_n),
        in_specs=[pl.BlockSpec((None, None, seq, ch), lambda b, g: (b, g, 0, 0))] + [const(m) for m in mats],
        out_specs=pl.BlockSpec((seq, ch), lambda b, g: (b, g)),
        out_shape=jax.ShapeDtypeStruct((batch * seq, g_n * ch), F32),
        compiler_params=_cparams(2),
        name="fourier_dense",
    )(pf, *mats)


def _merge_kernel(att_ref, sgu_ref, four_ref, ga_ref, gg_ref, gf_ref, wa_ref, ws_ref, wf_ref, wo_ref,
                  x_ref, gate_ref, g_ref, sh_ref, sc_ref, xo_ref, ho_ref):
    y = jax.nn.sigmoid(ga_ref[...].astype(F32)) * _dot(att_ref[...], wa_ref[...])
    y = y + jax.nn.sigmoid(gg_ref[...].astype(F32)) * _dot(sgu_ref[...], ws_ref[...])
    y = y + jax.nn.sigmoid(gf_ref[...].astype(F32)) * _dot(four_ref[...].astype(BF16), wf_ref[...])
    x = x_ref[...] + gate_ref[...] * _dot(y.astype(BF16), wo_ref[...])
    xo_ref[...] = x
    h = x * lax.rsqrt(jnp.mean(x * x, axis=-1, keepdims=True) + EPS) * g_ref[...]
    ho_ref[...] = (h * (1.0 + sc_ref[...]) + sh_ref[...]).astype(ho_ref.dtype)


def _merge_project_norm(att, sgu, four, p, w_pa, w_ps, w_pf, w_o, x2d, gate, g, shift, scale, *,
                        gate_block, rows_per_batch):
    m, k = att.shape
    d = w_pa.shape[1]
    tm = min(256, rows_per_batch)
    tpb = rows_per_batch // tm
    act = pl.BlockSpec((tm, k), lambda i: (i, 0))
    gcol = lambda off: pl.BlockSpec((tm, d), lambda i: (i, gate_block + off))
    row = pl.BlockSpec((tm, d), lambda i: (i, 0))
    wspec = _resident((k, d), lambda i: (0, 0))
    vec = pl.BlockSpec((None, 1, d), lambda i: (i // tpb, 0, 0))
    return pl.pallas_call(
        _merge_kernel,
        grid=(m // tm,),
        in_specs=[act, act, act, gcol(0), gcol(1), gcol(2), wspec, wspec, wspec,
                  _resident((d, d), lambda i: (0, 0)), row, vec,
                  pl.BlockSpec((1, d), lambda i: (0, 0)), vec, vec],
        out_specs=[row, row],
        out_shape=[jax.ShapeDtypeStruct((m, d), F32), jax.ShapeDtypeStruct((m, d), BF16)],
        compiler_params=_cparams(1),
        name="merge_outproj_norm",
    )(att, sgu, four, p, p, p, w_pa, w_ps, w_pf, w_o, x2d, gate, g.reshape(1, d), shift, scale)


def _rope_tables(n_tokens):
    n_rows = n_tokens // GRID_W
    row = jnp.broadcast_to(jnp.arange(n_rows, dtype=F32)[:, None], (n_rows, GRID_W)).reshape(-1)
    col = jnp.broadcast_to(jnp.arange(GRID_W, dtype=F32)[None, :], (n_rows, GRID_W)).reshape(-1)
    n_freq = ATT_HD // 4
    inv = ROPE_BASE ** (-jnp.arange(n_freq, dtype=F32) / n_freq)
    ar = row[:, None] * inv
    ac = col[:, None] * inv
    ang = jnp.concatenate([ar, ar, ac, ac] * (LANES // ATT_HD), axis=-1)
    cos, sin = jnp.cos(ang), jnp.sin(ang)
    first_half = (jnp.arange(LANES) % (ATT_HD // 2)) < (ATT_HD // 4)
    return cos, jnp.where(first_half, -sin, 0.0), jnp.where(first_half, 0.0, sin)


def _split_in_projection(w_in):
    depth, d, _ = w_in.shape
    qk = ATT_HEADS * ATT_HD
    v0 = 4 * qk
    su0 = v0 + ATT_HEADS * ATT_VD
    f0 = su0 + 2 * SGU_GROUPS * SGU_CH
    g0 = f0 + FOURIER_GROUPS * FOURIER_CH
    seg = lambda s: w_in[:, :, s * qk:(s + 1) * qk].reshape(depth, d, ATT_HEADS, ATT_HD)
    pair = lambda a, b: jnp.concatenate([seg(a), seg(b)], axis=-1).reshape(depth, d, 2 * qk)
    w_qk = jnp.concatenate([pair(0, 1), pair(2, 3)], axis=-1)
    w_rest = jnp.concatenate([w_in[:, :, g0:], w_in[:, :, v0:f0]], axis=-1)
    return w_qk.astype(BF16), w_rest.astype(BF16), w_in[:, :, f0:g0].astype(BF16)


def kernel(x, c, ctx, c_ctx, w_mod, b_mod, norm1_g, norm2_g, w_in, lambda_q1, lambda_k1, lambda_q2, lambda_k2,
           subln_g, sgu_norm_g, sgu_w, sgu_b, w_proj_att, w_proj_sgu, w_proj_fourier, w_out, w_mlp_in, w_mlp_out,
           final_g):
    batch, seq, d = x.shape
    ctx_len = ctx.shape[1]
    depth = w_mod.shape[0]
    in_width = w_in.shape[2]
    qk_width = 2 * ATT_HEADS * ATT_HD
    v_width = ATT_HEADS * ATT_VD
    sgu_width = SGU_GROUPS * SGU_CH
    four_width = FOURIER_GROUPS * FOURIER_CH
    assert in_width == 2 * qk_width + v_width + 2 * sgu_width + four_width + 3 * d
    assert seq % (LANES * F32_SUBLANES) == 0 and seq % GRID_W == 0 and ctx_len % 256 == 0
    gate_block = 0
    v_block = 3 * d // LANES
    su_block = (3 * d + v_width) // sgu_width
    assert (3 * d + v_width) % sgu_width == 0

    w_qk, w_rest, w_four = _split_in_projection(w_in)
    w_pa, w_ps, w_pf = (w.astype(BF16) for w in (w_proj_att, w_proj_sgu, w_proj_fourier))
    w_o, w_1, w_2 = (w.astype(BF16) for w in (w_out, w_mlp_in, w_mlp_out))
    sgu_w_b = sgu_w.astype(BF16)
    sgu_b_lanes = jnp.broadcast_to(sgu_b[..., None], sgu_b.shape + (SGU_CH,))

    cos, sin_up, sin_down = _rope_tables(seq)
    ctx_rows = min(1024, ctx_len)
    one_t, zero_t = jnp.ones((ctx_rows, LANES), F32), jnp.zeros((ctx_rows, LANES), F32)

    pad = jnp.zeros((F32_SUBLANES - batch - 1, d), F32)
    c_rows = jnp.concatenate([c, c_ctx[None, :], pad], axis=0)
    mod = _modulation(c_rows, w_mod, b_mod)

    xs =x.reshape(batch * seq, d)
    cs = ctx.reshape(batch * ctx_len, d)
    for l in range(depth):
        last = l == depth - 1
        lambda_init = 0.8 - 0.6 * math.exp(-0.3 * l)
        lam_rows = jnp.stack([lambda_q1[l], lambda_k1[l], lambda_q2[l], lambda_k2[l]]).astype(F32)
        lam_rows = jnp.pad(lam_rows, ((0, F32_SUBLANES - 4), (0, LANES - ATT_HD)))
        mod_x = mod[l, :batch].reshape(batch, 1, N_MOD, d)
        mod_c = mod[l, batch:batch + 1].reshape(1, 1, N_MOD, d)
        mx = [mod_x[:, :, i] for i in range(N_MOD)]
        mc = [mod_c[:, :, i] for i in range(N_MOD)]

        hx = _rmsnorm(xs, norm1_g[l], mx[0], mx[1], rows_per_batch=seq, out_dtype=BF16)
        hc = _rmsnorm(cs, norm1_g[l], mc[0], mc[1], rows_per_batch=batch * ctx_len, out_dtype=BF16)
        qk_x = _in_projection_qk(hx, w_qk[l], cos, sin_up, sin_down, rows_per_batch=seq)
        qk_c = _in_projection_qk(hc, w_qk[l], one_t, zero_t, zero_t, rows_per_batch=ctx_len)
        px = _in_projection_plain(hx, w_rest[l])
        pc = _in_projection_plain(hc, w_rest[l])
        fx = _in_projection_fourier(hx, w_four[l], batch=batch)

        att_x = _diff_attention(lam_rows, subln_g[l], qk_x, [(qk_x, px), (qk_c, pc)], batch=batch, q_len=seq,
                                key_lens=[seq, ctx_len], lambda_init=lambda_init, v_block=v_block)
        sgu_x = _spatial_gating(px, sgu_w_b[l], sgu_b_lanes[l], sgu_norm_g[l], su_block=su_block,
                                rows_per_batch=seq)
        four_x = _fourier_long(fx, batch=batch, seq=seq)

        if not last:
            fc = _in_projection_fourier(hc, w_four[l], batch=batch)
            att_c = _diff_attention(lam_rows, subln_g[l], qk_c, [(qk_c, pc)], batch=batch, q_len=ctx_len,
                                    key_lens=[ctx_len], lambda_init=lambda_init, v_block=v_block)
            sgu_c = _spatial_gating(pc, sgu_w_b[l], sgu_b_lanes[l], sgu_norm_g[l], su_block=su_block,
                                    rows_per_batch=ctx_len)
            four_c = _fourier_short(fc, batch=batch, seq=ctx_len)
            cs, hc2 = _merge_project_norm(att_c, sgu_c, four_c, pc, w_pa[l], w_ps[l], w_pf[l], w_o[l], cs,
                                          mc[2], norm2_g[l], mc[3], mc[4], gate_block=gate_block,
                                          rows_per_batch=batch * ctx_len)
            cs = _matmul_gated_residual(_matmul_relu2(hc2, w_1[l]), w_2[l], cs, mc[5],
                                        rows_per_batch=batch * ctx_len, name="mlp_out_residual")

        xs, hx2 = _merge_project_norm(att_x, sgu_x, four_x, px, w_pa[l], w_ps[l], w_pf[l], w_o[l], xs,
                                      mx[2], norm2_g[l], mx[3], mx[4], gate_block=gate_block,
                                      rows_per_batch=seq)
        xs = _matmul_gated_residual(_matmul_relu2(hx2, w_1[l]), w_2[l], xs, mx[5], rows_per_batch=seq,
                                    name="mlp_out_residual")

    out = _rmsnorm(xs, final_g, rows_per_batch=seq, out_dtype=x.dtype)
    return out.reshape(batch, seq, d)
```

```python
import functools
import math

import jax
import jax.numpy as jnp
from jax import lax
from jax.experimental import pallas as pl
from jax.experimental.pallas import tpu as pltpu

F32 = jnp.float32
BF16 = jnp.bfloat16

GRID_W = 64
ATT_HEADS = 8
ATT_HD = 64
ATT_VD = 2 * ATT_HD
ROPE_BASE = 10000.0
SGU_GROUPS = 8
SGU_CH = 128
SGU_CHUNK = 128
FOURIER_GROUPS = 4
FOURIER_CH = 256
N_MOD = 6
EPS = 1e-6
SUBLN_EPS = 1e-5

LANES = 128
BF16_SUBLANES = 16
F32_SUBLANES = 8
VMEM_LIMIT_BYTES = 56 * 2**20

LOG2E = 1.4426950408889634
Q_PRESCALE = (ATT_HD ** -0.5) * LOG2E
NEG_BIG = -1e30
MAX_SLAB = 64


def _cparams(n_axes):
    return pltpu.CompilerParams(dimension_semantics=("arbitrary",) * n_axes,
                                vmem_limit_bytes=VMEM_LIMIT_BYTES)


def _dot(a, b):
    return jnp.dot(a, b, preferred_element_type=F32)


def _resident(shape, index_map):
    return pl.BlockSpec(shape, index_map, pipeline_mode=pl.Buffered(1))


def _mod_kernel(c_ref, w_ref, b_ref, o_ref):
    a = c_ref[...]
    a = a * jax.nn.sigmoid(a)
    w = w_ref[...]
    a_hi = a.astype(BF16)
    a_lo = (a - a_hi.astype(F32)).astype(BF16)
    w_hi = w.astype(BF16)
    w_lo = (w - w_hi.astype(F32)).astype(BF16)
    o_ref[...] = _dot(a_hi, w_hi) + _dot(a_hi, w_lo) + _dot(a_lo, w_hi) + b_ref[...]


def _modulation(c_rows, w_mod, b_mod):
    depth, d, n = w_mod.shape
    tn = 512
    rows = c_rows.shape[0]
    return pl.pallas_call(
        _mod_kernel,
        grid=(depth, n // tn),
        in_specs=[
            pl.BlockSpec((rows, d), lambda l, j: (0, 0)),
            pl.BlockSpec((None, d, tn), lambda l, j: (l, 0, j)),
            pl.BlockSpec((None, 1, tn), lambda l, j: (l, 0, j)),
        ],
        out_specs=pl.BlockSpec((None, rows, tn), lambda l, j: (l, 0, j)),
        out_shape=jax.ShapeDtypeStruct((depth, rows, n), F32),
        compiler_params=_cparams(2),
        name="adaln_mod",
    )(c_rows, w_mod, b_mod.reshape(depth, 1, n))


def _norm_kernel(x_ref, g_ref, *rest, eps, modulated):
    x = x_ref[...]
    y = x * lax.rsqrt(jnp.mean(x * x, axis=-1, keepdims=True) + eps) * g_ref[...]
    if modulated:
        sh_ref, sc_ref, o_ref = rest
        y = y * (1.0 + sc_ref[...]) + sh_ref[...]
    else:
        (o_ref,) = rest
    o_ref[...] = y.astype(o_ref.dtype)


def _rmsnorm(x2d, g, shift=None, scale=None, *, rows_per_batch, out_dtype):
    m, d = x2d.shape
    tm = min(512, rows_per_batch)
    tpb = rows_per_batch // tm
    modulated = shift is not None
    in_specs = [pl.BlockSpec((tm, d), lambda i: (i, 0)), pl.BlockSpec((1, d), lambda i: (0, 0))]
    args = [x2d, g.reshape(1, d)]
    if modulated:
        vec = pl.BlockSpec((None, 1, d), lambda i: (i // tpb, 0, 0))
        in_specs += [vec, vec]
        args += [shift, scale]
    return pl.pallas_call(
        functools.partial(_norm_kernel, eps=EPS, modulated=modulated),
        grid=(m // tm,),
        in_specs=in_specs,
        out_specs=pl.BlockSpec((tm, d), lambda i: (i, 0)),
        out_shape=jax.ShapeDtypeStruct((m, d), out_dtype),
        compiler_params=_cparams(1),
        name="rmsnorm_mod" if modulated else "rmsnorm",
    )(*args)


def _mm_rope_kernel(a_ref, w_ref, cos_ref, sa_ref, sb_ref, o_ref, *, n_q_tiles):
    acc = _dot(a_ref[...], w_ref[...])
    cos, sa, sb = cos_ref[...], sa_ref[...], sb_ref[...]
    scale = jnp.where(pl.program_id(1) < n_q_tiles, Q_PRESCALE, 1.0).astype(F32)
    for c in range(acc.shape[1] // LANES):
        xc = acc[:, c * LANES:(c + 1) * LANES]
        up = pltpu.roll(xc, LANES - 16, 1)
        down = pltpu.roll(xc, 16, 1)
        r = xc * cos + up * sa + down * sb
        o_ref[:, c * LANES:(c + 1) * LANES] = (r * scale).astype(o_ref.dtype)


def _in_projection_qk(h, w, cos, sin_up, sin_down, *, rows_per_batch):
    m, k = h.shape
    n = w.shape[1]
    tm = min(1024, rows_per_batch)
    tn = 1024
    tpb = rows_per_batch // tm
    table = pl.BlockSpec((tm, LANES), lambda i, j: (i % tpb, 0))
    return pl.pallas_call(
        functools.partial(_mm_rope_kernel, n_q_tiles=n // (2 * tn)),
        grid=(m // tm, n // tn),
        in_specs=[
            pl.BlockSpec((tm, k), lambda i, j: (i, 0)),
            pl.BlockSpec((k, tn), lambda i, j: (0, j)),
            table, table, table,
        ],
        out_specs=pl.BlockSpec((tm, tn), lambda i, j: (i, j)),
        out_shape=jax.ShapeDtypeStruct((m, n), BF16),
        compiler_params=_cparams(2),
        name="in_proj_qk",
    )(h, w, cos, sin_up, sin_down)


def _mm_kernel(a_ref, w_ref, o_ref):
    o_ref[...] = _dot(a_ref[...], w_ref[...]).astype(o_ref.dtype)


def _in_projection_plain(h, w):
    m, k = h.shape
    n = w.shape[1]
    tm = min(1024, m)
    tn = 1024
    return pl.pallas_call(
        _mm_kernel,
        grid=(m // tm, n // tn),
        in_specs=[pl.BlockSpec((tm, k), lambda i, j: (i, 0)), pl.BlockSpec((k, tn), lambda i, j: (0, j))],
        out_specs=pl.BlockSpec((tm, tn), lambda i, j: (i, j)),
        out_shape=jax.ShapeDtypeStruct((m, n), BF16),
        compiler_params=_cparams(2),
        name="in_proj_plain",
    )(h, w)


def _mm_groups_kernel(a_ref, w_ref, o_ref):
    acc = _dot(a_ref[...], w_ref[...])
    for g in range(o_ref.shape[0]):
        o_ref[g] = acc[:, g * o_ref.shape[2]:(g + 1) * o_ref.shape[2]].astype(o_ref.dtype)


def _in_projection_fourier(h, w, *, batch):
    m, k = h.shape
    n = w.shape[1]
    seq = m // batch
    tm = min(1024, seq)
    tpb = seq // tm
    return pl.pallas_call(
        _mm_groups_kernel,
        grid=(m // tm,),
        in_specs=[pl.BlockSpec((tm, k), lambda i: (i, 0)), _resident((k, n), lambda i: (0, 0))],
        out_specs=pl.BlockSpec((None, FOURIER_GROUPS, tm, FOURIER_CH), lambda i: (i // tpb, 0, i % tpb, 0)),
        out_shape=jax.ShapeDtypeStruct((batch, FOURIER_GROUPS, seq, FOURIER_CH), BF16),
        compiler_params=_cparams(1),
        name="in_proj_fourier",
    )(h, w)


def _mm_relu2_kernel(a_ref, w_ref, o_ref):
    acc = jnp.maximum(_dot(a_ref[...], w_ref[...]), 0.0)
    o_ref[...] = (acc * acc).astype(o_ref.dtype)


def _matmul_relu2(a, w):
    m, k = a.shape
    n = w.shape[1]
    tm = min(1024, m)
    tn = 1024
    return pl.pallas_call(
        _mm_relu2_kernel,
        grid=(m // tm, n // tn),
        in_specs=[pl.BlockSpec((tm, k), lambda i, j: (i, 0)), pl.BlockSpec((k, tn), lambda i, j: (0, j))],
        out_specs=pl.BlockSpec((tm, tn), lambda i, j: (i, j)),
        out_shape=jax.ShapeDtypeStruct((m, n), BF16),
        compiler_params=_cparams(2),
        name="mlp_in_relu2",
    )(a, w)


def _mm_resid_kernel(a_ref, w_ref, x_ref, gate_ref, o_ref):
    o_ref[...] = x_ref[...] + gate_ref[...] * _dot(a_ref[...], w_ref[...])


def _matmul_gated_residual(a, w, x2d, gate, *, rows_per_batch, name):
    m, k = a.shape
    n = w.shape[1]
    tm = min(1024, rows_per_batch)
    tn = 256
    tpb = rows_per_batch // tm
    return pl.pallas_call(
        _mm_resid_kernel,
        grid=(m // tm, n // tn),
        in_specs=[
            pl.BlockSpec((tm, k), lambda i, j: (i, 0)),
            pl.BlockSpec((k, tn), lambda i, j: (0, j)),
            pl.BlockSpec((tm, tn), lambda i, j: (i, j)),
            pl.BlockSpec((None, 1, tn), lambda i, j: (i // tpb, 0, j)),
        ],
        out_specs=pl.BlockSpec((tm, tn), lambda i, j: (i, j)),
        out_shape=jax.ShapeDtypeStruct((m, n), F32),
        compiler_params=_cparams(2),
        name=name,
    )(a, w, x2d, gate)


def _attn_kernel(lam_ref, g_ref, q_ref, *refs, key_lens, tk, tq, lambda_init):
    n_src = len(key_lens)
    k_refs = refs[0:2 * n_src:2]
    v_refs = refs[1:2 * n_src:2]
    o_ref = refs[2 * n_src]
    k_all, vt_all, rhs_ref, s_ref, p_ref, mt_ref, alpha_ref, m_ref, acc_ref = refs[2 * n_src + 1:]
    q_blocks = q_ref.shape[0] // tq
    n_keys = sum(key_lens)
    n_tiles = n_keys // tk
    chunk = 256

    @pl.when(pl.program_id(2) == 0)
    def _():
        base = 0
        for k_ref, v_ref, klen in zip(k_refs, v_refs, key_lens):
            def cp(c, _, k_ref=k_ref, v_ref=v_ref, base=base):
                off = pl.multiple_of(c * chunk, chunk)
                k_all[pl.ds(base + off, chunk), :] = k_ref[pl.ds(off, chunk), :]
                vt_all[0:ATT_VD, pl.ds(base + off, chunk)] = (
                    v_ref[pl.ds(off, chunk), :].astype(F32).T.astype(BF16))
                return 0
            lax.fori_loop(0, klen // chunk, cp, 0)
            base += klen
        vt_all[ATT_VD:, :] = jnp.ones((vt_all.shape[0] - ATT_VD, n_keys), BF16)

    for u in range(q_blocks):
        qt = q_ref[u * tq:(u + 1) * tq, :].astype(F32).T
        row = lax.broadcasted_iota(jnp.int32, qt.shape, 0)
        rhs_ref[u, 0] = jnp.where(row < ATT_HD, qt, 0.0).astype(BF16)
        rhs_ref[u, 1] = jnp.where(row >= ATT_HD, qt, 0.0).astype(BF16)
    m_ref[...] = jnp.full(m_ref.shape, NEG_BIG, F32)
    stat_shape = m_ref.shape[2:]
    acc_ref[...] = jnp.zeros(acc_ref.shape, F32)

    def scores(g):
        u, t = divmod(g, n_tiles)
        par = g % 2
        k_tile = k_all[t * tk:(t + 1) * tk, :]
        for mp in range(2):
            s = _dot(k_tile, rhs_ref[u, mp])
            s_ref[par, mp] = s
            slab = jnp.max(s.reshape(tk // MAX_SLAB, MAX_SLAB, tq), axis=0)
            mt_ref[par, mp] = jnp.broadcast_to(jnp.max(slab, axis=0, keepdims=True), stat_shape)

    def softmax(g):
        u = g // n_tiles
        par = g % 2
        for mp in range(2):
            m_old = m_ref[u, mp]
            m_new = jnp.maximum(m_old, mt_ref[par, mp])
            alpha_ref[par, mp] = jnp.exp2(m_old - m_new)
            m_ref[u, mp] = m_new
            s = s_ref[par, mp].reshape(tk // F32_SUBLANES, F32_SUBLANES, tq)
            p_ref[par, mp] = jnp.exp2(s - m_new[None]).reshape(tk, tq).astype(BF16)

    def weighted_values(g):
        u, t = divmod(g, n_tiles)
        par = g % 2
        vt_tile = vt_all[:, t * tk:(t + 1) * tk]
        for mp in range(2):
            acc = acc_ref[u, mp].reshape(-1, F32_SUBLANES, tq) * alpha_ref[par, mp][None]
            acc_ref[u, mp] = acc.reshape(acc_ref.shape[2:]) + _dot(vt_tile, p_ref[par, mp])
        if t == n_tiles - 1:
            finalize(u)

    def finalize(u):
        lv = lam_ref[...]
        lam = (jnp.exp(jnp.sum(lv[0:1] * lv[1:2], axis=-1, keepdims=True))
               - jnp.exp(jnp.sum(lv[2:3] * lv[3:4], axis=-1, keepdims=True)) + lambda_init)
        a1, a2 = acc_ref[u, 0], acc_ref[u, 1]
        o = (a1[:ATT_VD] * (1.0 / a1[ATT_VD:ATT_VD + 1])
             - lam * (a2[:ATT_VD] * (1.0 / a2[ATT_VD:ATT_VD + 1])))
        o = o * lax.rsqrt(jnp.mean(o * o, axis=0, keepdims=True) + SUBLN_EPS)
        o_ref[u * tq:(u + 1) * tq, :] = (o.T * g_ref[...] * (1.0 - lambda_init)).astype(o_ref.dtype)

    total = q_blocks * n_tiles
    scores(0)
    if total > 1:
        scores(1)
    softmax(0)
    for g in range(total):
        if g + 2 < total:
            scores(g + 2)
        if g + 1 < total:
            softmax(g + 1)
        weighted_values(g)


def _key_tile(n_keys, cap):
    best = 256
    for t in range(256, cap + 1, 256):
        if n_keys % t == 0:
            best = t
    return best


def _diff_attention(lam_rows, subln_g, q_src, kv_srcs, *, batch, q_len, key_lens, lambda_init, v_block):
    h = ATT_HEADS
    tq = min(256, q_len)
    q_blocks = 2 if (q_len // tq) % 2 == 0 else 1
    n_keys = sum(key_lens)
    tk = _key_tile(n_keys, 1536)
    nq = q_len // (tq * q_blocks)
    in_specs = [
        pl.BlockSpec((F32_SUBLANES, LANES), lambda b, hd, i: (0, 0)),
        pl.BlockSpec((1, ATT_VD), lambda b, hd, i: (0, 0)),
        pl.BlockSpec((q_blocks * tq, LANES), lambda b, hd, i: (b * nq + i, hd)),
    ]
    args = [lam_rows, subln_g.reshape(1, ATT_VD), q_src]
    for (k_src, v_src), klen in zip(kv_srcs, key_lens):
        in_specs += [pl.BlockSpec((klen, LANES), lambda b, hd, i: (b, h + hd)),
                     pl.BlockSpec((klen, LANES), lambda b, hd, i: (b, v_block + hd))]
        args += [k_src, v_src]
    stat = (F32_SUBLANES, tq)
    v_rows = ATT_VD + BF16_SUBLANES
    scratch = [pltpu.VMEM((n_keys, LANES), BF16), pltpu.VMEM((v_rows, n_keys), BF16),
               pltpu.VMEM((q_blocks, 2, LANES, tq), BF16), pltpu.VMEM((2, 2, tk, tq), F32),
               pltpu.VMEM((2, 2, tk, tq), BF16), pltpu.VMEM((2, 2) + stat, F32),
               pltpu.VMEM((2, 2) + stat, F32), pltpu.VMEM((q_blocks, 2) + stat, F32),
               pltpu.VMEM((q_blocks, 2, v_rows, tq), F32)]
    return pl.pallas_call(
        functools.partial(_attn_kernel, key_lens=tuple(key_lens), tk=tk, tq=tq, lambda_init=lambda_init),
        grid=(batch, h, nq),
        in_specs=in_specs,
        out_specs=pl.BlockSpec((q_blocks * tq, ATT_VD), lambda b, hd, i: (b * nq + i, hd)),
        out_shape=jax.ShapeDtypeStruct((batch * q_len, h * ATT_VD), BF16),
        scratch_shapes=scratch,
        compiler_params=_cparams(3),
        name="diff_attention",
    )(*args)


def _gelu(x):
    return 0.5 * x * (1.0 + lax.erf(x * (2.0 ** -0.5)))


def _sgu_kernel(su_ref, sv_ref, w_ref, b_ref, g_ref, o_ref):
    for c in range(su_ref.shape[0] // SGU_CHUNK):
        rows = slice(c * SGU_CHUNK, (c + 1) * SGU_CHUNK)
        u = _gelu(su_ref[rows, :].astype(F32))
        v = _gelu(sv_ref[rows, :].astype(F32))
        v = (v * lax.rsqrt(jnp.mean(v * v, axis=-1, keepdims=True) + EPS) * g_ref[...]).astype(BF16)
        for g in range(SGU_GROUPS):
            cols = slice(g * SGU_CH, (g + 1) * SGU_CH)
            mixed = _dot(w_ref[g], v[:, cols]) + b_ref[g]
            o_ref[rows, cols] = (u[:, cols] * mixed).astype(o_ref.dtype)


def _spatial_gating(p, w_s, b_s_lanes, g_v, *, su_block, rows_per_batch):
    m = p.shape[0]
    width = SGU_GROUPS * SGU_CH
    tm = min(512, rows_per_batch)
    return pl.pallas_call(
        _sgu_kernel,
        grid=(m // tm,),
        in_specs=[
            pl.BlockSpec((tm, width), lambda i: (i, su_block)),
            pl.BlockSpec((tm, width), lambda i: (i, su_block + 1)),
            pl.BlockSpec(w_s.shape, lambda i: (0, 0, 0)),
            pl.BlockSpec(b_s_lanes.shape, lambda i: (0, 0, 0)),
            pl.BlockSpec((1, width), lambda i: (0, 0)),
        ],
        out_specs=pl.BlockSpec((tm, width), lambda i: (i, 0)),
        out_shape=jax.ShapeDtypeStruct((m, width), BF16),
        compiler_params=_cparams(1),
        name="spatial_gating",
    )(p, p, w_s, b_s_lanes, g_v.reshape(1, width))


def _dft_cos_sin(n, scale):
    j = jnp.arange(n, dtype=jnp.int32)
    ang = ((j[:, None] * j[None, :]) % n).astype(F32) * (2.0 * math.pi / n)
    return jnp.cos(ang) * scale, jnp.sin(ang) * scale


def _fourier_stage1_kernel(x_ref, w_ref, tc_ref, ts_ref, o_ref):
    n2 = x_ref.shape[0]
    y = _dot(w_ref[...], x_ref[...])
    br, bi = y[:n2], y[n2:]
    reps = FOURIER_CH // LANES

    def widen(t):
        parts = [t[:, q * LANES:(q + 1) * LANES] for q in range(t.shape[1] // LANES)]
        return jnp.concatenate([p for p in parts for _ in range(reps)], axis=1)

    tc, ts = widen(tc_ref[...]), widen(ts_ref[...])
    o_ref[0] = (br * tc + bi * ts).astype(o_ref.dtype)
    o_ref[1] = (bi * tc - br * ts).astype(o_ref.dtype)


def _fourier_stage2_kernel(b_ref, w3_ref, cc_ref, sc_ref, o_ref):
    _, r, n1, ch = b_ref.shape
    w3, cc, sc = w3_ref[...], cc_ref[...], sc_ref[...]
    for j in range(r):
        bj = jnp.concatenate([b_ref[0, j], b_ref[1, j]], axis=0)
        y = _dot(w3, bj)
        out = _dot(y[:n1].astype(BF16), cc) + _dot(y[n1:].astype(BF16), sc)
        o_ref[:, j, :] = out


def _fourier_long(pf, *, batch, seq):
    n1 = LANES
    n2 = seq // n1
    ch = FOURIER_CH
    g_n = FOURIER_GROUPS
    r1 = min(16, n1)
    r2 = F32_SUBLANES

    c2, s2 = _dft_cos_sin(n2, n2 ** -0.5)
    w1 = jnp.concatenate([c2, -s2], axis=0).astype(BF16)

    kk = jnp.arange(n2, dtype=jnp.int32)[:, None] * jnp.arange(n1, dtype=jnp.int32)[None, :]
    ang = (kk % seq).astype(F32) * (2.0 * math.pi / seq)
    tw_c = jnp.broadcast_to(jnp.cos(ang)[:, :, None], (n2, n1, LANES)).reshape(n2, n1 * LANES)
    tw_s = jnp.broadcast_to(jnp.sin(ang)[:, :, None], (n2, n1, LANES)).reshape(n2, n1 * LANES)

    c1, s1 = _dft_cos_sin(n1, n1 ** -0.5)
    k3 = jnp.concatenate([jnp.concatenate([c1, s1], axis=1),
                          jnp.concatenate([-s1, c1], axis=1)], axis=0).astype(BF16)
    cc, sc = _dft_cos_sin(ch, ch ** -0.5)
    cc, sc = cc.astype(BF16), sc.astype(BF16)

    x2 = pf.reshape(batch, g_n, n2, n1 * ch)
    stage1 = pl.pallas_call(
        _fourier_stage1_kernel,
        grid=(batch, g_n, n1 // r1),
        in_specs=[
            pl.BlockSpec((None, None, n2, r1 * ch), lambda b, g, i: (b, g, 0, i)),
            _resident(w1.shape, lambda b, g, i: (0, 0)),
            pl.BlockSpec((n2, r1 * LANES), lambda b, g, i: (0, i)),
            pl.BlockSpec((n2, r1 * LANES), lambda b, g, i: (0, i)),
        ],
        out_specs=pl.BlockSpec((None, None, 2, n2, r1 * ch), lambda b, g, i: (b, g, 0, 0, i)),
        out_shape=jax.ShapeDtypeStruct((batch, g_n, 2, n2, n1 * ch), BF16),
        compiler_params=_cparams(3),
        name="fourier_stage1",
    )(x2, w1, tw_c, tw_s)
    stage1 = stage1.reshape(batch, g_n, 2, n2, n1, ch)

    out = pl.pallas_call(
        _fourier_stage2_kernel,
        grid=(batch, g_n, n2 // r2),
        in_specs=[
            pl.BlockSpec((None, None, 2, r2, n1, ch), lambda b, g, i: (b, g, 0, i, 0, 0)),
            _resident(k3.shape, lambda b, g, i: (0, 0)),
            _resident(cc.shape, lambda b, g, i: (0, 0)),
            _resident(sc.shape, lambda b, g, i: (0, 0)),
        ],
        out_specs=pl.BlockSpec((None, n1, r2, ch), lambda b, g, i: (b, 0, i, g)),
        out_shape=jax.ShapeDtypeStruct((batch, n1, n2, g_n * ch), F32),
        compiler_params=_cparams(3),
        name="fourier_stage2",
    )(stage1, k3, cc, sc)
    return out.reshape(batch * seq, g_n * ch)


def _fourier_dense_kernel(x_ref, cl_ref, sl_ref, cc_ref, sc_ref, o_ref):
    x = x_ref[...]
    a = _dot(cl_ref[...], x).astype(BF16)
    b = _dot(sl_ref[...], x).astype(BF16)
    o_ref[...] = _dot(a, cc_ref[...]) - _dot(b, sc_ref[...])


def _fourier_short(pf, *, batch, seq):
    ch = FOURIER_CH
    g_n = FOURIER_GROUPS
    cl, sl = _dft_cos_sin(seq, seq ** -0.5)
    cc, sc = _dft_cos_sin(ch, ch ** -0.5)
    const = lambda a: _resident(a.shape, lambda b, g: (0, 0))
    mats = [m.astype(BF16) for m in (cl, sl, cc, sc)]
    return pl.pallas_call(
        _fourier_dense_kernel,
        grid=(batch, g_n),
        in_specs=[pl.BlockSpec((None, None, seq, ch), lambda b, g: (b, g, 0, 0))] + [const(m) for m in mats],
        out_specs=pl.BlockSpec((seq, ch), lambda b, g: (b, g)),
        out_shape=jax.ShapeDtypeStruct((batch * seq, g_n * ch), F32),
        compiler_params=_cparams(2),
        name="fourier_dense",
    )(pf, *mats)


def _merge_kernel(att_ref, sgu_ref, four_ref, ga_ref, gg_ref, gf_ref, wa_ref, ws_ref, wf_ref, wo_ref,
                  x_ref, gate_ref, g_ref, sh_ref, sc_ref, xo_ref, ho_ref):
    y = jax.nn.sigmoid(ga_ref[...].astype(F32)) * _dot(att_ref[...], wa_ref[...])
    y = y + jax.nn.sigmoid(gg_ref[...].astype(F32)) * _dot(sgu_ref[...], ws_ref[...])
    y = y + jax.nn.sigmoid(gf_ref[...].astype(F32)) * _dot(four_ref[...].astype(BF16), wf_ref[...])
    x = x_ref[...] + gate_ref[...] * _dot(y.astype(BF16), wo_ref[...])
    xo_ref[...] = x
    h = x * lax.rsqrt(jnp.mean(x * x, axis=-1, keepdims=True) + EPS) * g_ref[...]
    ho_ref[...] = (h * (1.0 + sc_ref[...]) + sh_ref[...]).astype(ho_ref.dtype)


def _merge_project_norm(att, sgu, four, p, w_pa, w_ps, w_pf, w_o, x2d, gate, g, shift, scale, *,
                        gate_block, rows_per_batch):
    m, k = att.shape
    d = w_pa.shape[1]
    tm = min(256, rows_per_batch)
    tpb = rows_per_batch // tm
    act = pl.BlockSpec((tm, k), lambda i: (i, 0))
    gcol = lambda off: pl.BlockSpec((tm, d), lambda i: (i, gate_block + off))
    row = pl.BlockSpec((tm, d), lambda i: (i, 0))
    wspec = _resident((k, d), lambda i: (0, 0))
    vec = pl.BlockSpec((None, 1, d), lambda i: (i // tpb, 0, 0))
    return pl.pallas_call(
        _merge_kernel,
        grid=(m // tm,),
        in_specs=[act, act, act, gcol(0), gcol(1), gcol(2), wspec, wspec, wspec,
                  _resident((d, d), lambda i: (0, 0)), row, vec,
                  pl.BlockSpec((1, d), lambda i: (0, 0)), vec, vec],
        out_specs=[row, row],
        out_shape=[jax.ShapeDtypeStruct((m, d), F32), jax.ShapeDtypeStruct((m, d), BF16)],
        compiler_params=_cparams(1),
        name="merge_outproj_norm",
    )(att, sgu, four, p, p, p, w_pa, w_ps, w_pf, w_o, x2d, gate, g.reshape(1, d), shift, scale)


def _rope_tables(n_tokens):
    n_rows = n_tokens // GRID_W
    row = jnp.broadcast_to(jnp.arange(n_rows, dtype=F32)[:, None], (n_rows, GRID_W)).reshape(-1)
    col = jnp.broadcast_to(jnp.arange(GRID_W, dtype=F32)[None, :], (n_rows, GRID_W)).reshape(-1)
    n_freq = ATT_HD // 4
    inv = ROPE_BASE ** (-jnp.arange(n_freq, dtype=F32) / n_freq)
    ar = row[:, None] * inv
    ac = col[:, None] * inv
    ang = jnp.concatenate([ar, ar, ac, ac] * (LANES // ATT_HD), axis=-1)
    cos, sin = jnp.cos(ang), jnp.sin(ang)
    first_half = (jnp.arange(LANES) % (ATT_HD // 2)) < (ATT_HD // 4)
    return cos, jnp.where(first_half, -sin, 0.0), jnp.where(first_half, 0.0, sin)


def _split_in_projection(w_in):
    d = w_in.shape[0]
    qk = ATT_HEADS * ATT_HD
    v0 = 4 * qk
    su0 = v0 + ATT_HEADS * ATT_VD
    f0 = su0 + 2 * SGU_GROUPS * SGU_CH
    g0 = f0 + FOURIER_GROUPS * FOURIER_CH
    seg = lambda s: w_in[:, s * qk:(s + 1) * qk].astype(BF16).reshape(d, ATT_HEADS, ATT_HD)
    pair = lambda a, b: jnp.concatenate([seg(a), seg(b)], axis=-1).reshape(d, 2 * qk)
    w_qk = jnp.concatenate([pair(0, 1), pair(2, 3)], axis=-1)
    w_rest = jnp.concatenate([w_in[:, g0:].astype(BF16), w_in[:, v0:f0].astype(BF16)], axis=-1)
    return w_qk, w_rest, w_in[:, f0:g0].astype(BF16)


def kernel(x, c, ctx, c_ctx, w_mod, b_mod, norm1_g, norm2_g, w_in, lambda_q1, lambda_k1, lambda_q2, lambda_k2,
           subln_g, sgu_norm_g, sgu_w, sgu_b, w_proj_att, w_proj_sgu, w_proj_fourier, w_out, w_mlp_in, w_mlp_out,
           final_g):
    batch, seq, d = x.shape
    ctx_len = ctx.shape[1]
    depth = w_mod.shape[0]
    in_width = w_in.shape[2]
    qk_width = 2 * ATT_HEADS * ATT_HD
    v_width = ATT_HEADS * ATT_VD
    sgu_width = SGU_GROUPS * SGU_CH
    four_width = FOURIER_GROUPS * FOURIER_CH
    assert in_width == 2 * qk_width + v_width + 2 * sgu_width + four_width + 3 * d
    assert seq % (LANES * F32_SUBLANES) == 0 and seq % GRID_W == 0 and ctx_len % 256 == 0
    gate_block = 0
    v_block = 3 * d // LANES
    su_block = (3 * d + v_width) // sgu_width
    assert (3 * d + v_width) % sgu_width == 0

    sgu_b_lanes = jnp.broadcast_to(sgu_b[..., None], sgu_b.shape + (SGU_CH,))

    cos, sin_up, sin_down = _rope_tables(seq)
    ctx_rows = min(1024, ctx_len)
    one_t, zero_t = jnp.ones((ctx_rows, LANES), F32), jnp.zeros((ctx_rows, LANES), F32)

    pad = jnp.zeros((F32_SUBLANES - batch - 1, d), F32)
    c_rows = jnp.concatenate([c, c_ctx[None, :], pad], axis=0)
    mod = _modulation(c_rows, w_mod, b_mod)

    xs = x.reshape(batch * seq, d)
    cs = ctx.reshape(batch * ctx_len, d)
    for l in range(depth):
        last = l == depth - 1
        lambda_init = 0.8 - 0.6 * math.exp(-0.3 * l)
        lam_rows = jnp.stack([lambda_q1[l], lambda_k1[l], lambda_q2[l], lambda_k2[l]]).astype(F32)
        lam_rows = jnp.pad(lam_rows, ((0, F32_SUBLANES - 4), (0, LANES - ATT_HD)))
        mod_x = mod[l, :batch].reshape(batch, 1, N_MOD, d)
        mod_c = mod[l, batch:batch + 1].reshape(1, 1, N_MOD, d)
        mx = [mod_x[:, :, i] for i in range(N_MOD)]
        mc = [mod_c[:, :, i] for i in range(N_MOD)]

        w_qk, w_rest, w_four = _split_in_projection(w_in[l])
        w_pa, w_ps, w_pf = (w[l].astype(BF16) for w in (w_proj_att, w_proj_sgu, w_proj_fourier))
        w_o, w_1, w_2 = (w[l].astype(BF16) for w in (w_out, w_mlp_in, w_mlp_out))
        sgu_w_b = sgu_w[l].astype(BF16)

        hx = _rmsnorm(xs, norm1_g[l], mx[0], mx[1], rows_per_batch=seq, out_dtype=BF16)
        hc = _rmsnorm(cs, norm1_g[l], mc[0], mc[1], rows_per_batch=batch * ctx_len, out_dtype=BF16)
        qk_x = _in_projection_qk(hx, w_qk, cos, sin_up, sin_down, rows_per_batch=seq)
        qk_c = _in_projection_qk(hc, w_qk, one_t, zero_t, zero_t, rows_per_batch=ctx_len)
        px = _in_projection_plain(hx, w_rest)
        pc = _in_projection_plain(hc, w_rest)
        fx = _in_projection_fourier(hx, w_four, batch=batch)

        att_x = _diff_attention(lam_rows, subln_g[l], qk_x, [(qk_x, px), (qk_c, pc)], batch=batch, q_len=seq,
                                key_lens=[seq, ctx_len], lambda_init=lambda_init, v_block=v_block)
        sgu_x = _spatial_gating(px, sgu_w_b, sgu_b_lanes[l], sgu_norm_g[l], su_block=su_block,
                                rows_per_batch=seq)
        four_x = _fourier_long(fx, batch=batch, seq=seq)

        if not last:
            fc = _in_projection_fourier(hc, w_four, batch=batch)
            att_c = _diff_attention(lam_rows, subln_g[l], qk_c, [(qk_c, pc)], batch=batch, q_len=ctx_len,
                                    key_lens=[ctx_len], lambda_init=lambda_init, v_block=v_block)
            sgu_c = _spatial_gating(pc, sgu_w_b, sgu_b_lanes[l], sgu_norm_g[l], su_block=su_block,
                                    rows_per_batch=ctx_len)
            four_c = _fourier_short(fc, batch=batch, seq=ctx_len)
            cs, hc2 = _merge_project_norm(att_c, sgu_c, four_c, pc, w_pa, w_ps, w_pf, w_o, cs,
                                          mc[2], norm2_g[l], mc[3], mc[4], gate_block=gate_block,
                                          rows_per_batch=batch * ctx_len)
            cs = _matmul_gated_residual(_matmul_relu2(hc2, w_1), w_2, cs, mc[5],
                                        rows_per_batch=batch * ctx_len, name="mlp_out_residual")

        xs, hx2 = _merge_project_norm(att_x, sgu_x, four_x, px, w_pa, w_ps, w_pf, w_o, xs,
                                      mx[2], norm2_g[l], mx[3], mx[4], gate_block=gate_block,
                                      rows_per_batch=seq)
        xs = _matmul_gated_residual(_matmul_relu2(hx2, w_1), w_2, xs, mx[5], rows_per_batch=seq,
                                    name="mlp_out_residual")

    out = _rmsnorm(xs, final_g, rows_per_batch=seq, out_dtype=x.dtype)
    return out.reshape(batch, seq, d)
```

```python
import functools
import math

import jax
import jax.numpy as jnp
from jax import lax
from jax.experimental import pallas as pl
from jax.experimental.pallas import tpu as pltpu

F32 = jnp.float32
BF16 = jnp.bfloat16

GRID_W = 64
ATT_HEADS = 8
ATT_HD = 64
ATT_VD = 2 * ATT_HD
ROPE_BASE = 10000.0
SGU_GROUPS = 8
SGU_CH = 128
SGU_CHUNK = 128
FOURIER_GROUPS = 4
FOURIER_CH = 256
N_MOD = 6
EPS = 1e-6
SUBLN_EPS = 1e-5

LANES = 128
BF16_SUBLANES = 16
F32_SUBLANES = 8
VMEM_LIMIT_BYTES = 56 * 2**20

LOG2E = 1.4426950408889634
Q_PRESCALE = (ATT_HD ** -0.5) * LOG2E
NEG_BIG = -1e30
MAX_SLAB = 64


def _cparams(n_axes):
    return pltpu.CompilerParams(dimension_semantics=("arbitrary",) * n_axes,
                                vmem_limit_bytes=VMEM_LIMIT_BYTES)


def _dot(a, b):
    return jnp.dot(a, b, preferred_element_type=F32)


def _resident(shape, index_map):
    return pl.BlockSpec(shape, index_map, pipeline_mode=pl.Buffered(1))


def _mod_kernel(c_ref, w_ref, b_ref, o_ref):
    a = c_ref[...]
    a = a * jax.nn.sigmoid(a)
    w = w_ref[...]
    a_hi = a.astype(BF16)
    a_lo = (a - a_hi.astype(F32)).astype(BF16)
    w_hi = w.astype(BF16)
    w_lo = (w - w_hi.astype(F32)).astype(BF16)
    o_ref[...] = _dot(a_hi, w_hi) + _dot(a_hi, w_lo) + _dot(a_lo, w_hi) + b_ref[...]


def _modulation(c_rows, w_mod, b_mod):
    depth, d, n = w_mod.shape
    tn = 512
    rows = c_rows.shape[0]
    return pl.pallas_call(
        _mod_kernel,
        grid=(depth, n // tn),
        in_specs=[
            pl.BlockSpec((rows, d), lambda l, j: (0, 0)),
            pl.BlockSpec((None, d, tn), lambda l, j: (l, 0, j)),
            pl.BlockSpec((None, 1, tn), lambda l, j: (l, 0, j)),
        ],
        out_specs=pl.BlockSpec((None, rows, tn), lambda l, j: (l, 0, j)),
        out_shape=jax.ShapeDtypeStruct((depth, rows, n), F32),
        compiler_params=_cparams(2),
        name="adaln_mod",
    )(c_rows, w_mod, b_mod.reshape(depth, 1, n))


def _norm_kernel(x_ref, g_ref, *rest, eps, modulated):
    x = x_ref[...]
    y = x * lax.rsqrt(jnp.mean(x * x, axis=-1, keepdims=True) + eps) * g_ref[...]
    if modulated:
        sh_ref, sc_ref, o_ref = rest
        y = y * (1.0 + sc_ref[...]) + sh_ref[...]
    else:
        (o_ref,) = rest
    o_ref[...] = y.astype(o_ref.dtype)


def _rmsnorm(x2d, g, shift=None, scale=None, *, rows_per_batch, out_dtype):
    m, d = x2d.shape
    tm = min(512, rows_per_batch)
    tpb = rows_per_batch // tm
    modulated = shift is not None
    in_specs = [pl.BlockSpec((tm, d), lambda i: (i, 0)), pl.BlockSpec((1, d), lambda i: (0, 0))]
    args = [x2d, g.reshape(1, d)]
    if modulated:
        vec = pl.BlockSpec((None, 1, d), lambda i: (i // tpb, 0, 0))
        in_specs += [vec, vec]
        args += [shift, scale]
    return pl.pallas_call(
        functools.partial(_norm_kernel, eps=EPS, modulated=modulated),
        grid=(m // tm,),
        in_specs=in_specs,
        out_specs=pl.BlockSpec((tm, d), lambda i: (i, 0)),
        out_shape=jax.ShapeDtypeStruct((m, d), out_dtype),
        compiler_params=_cparams(1),
        name="rmsnorm_mod" if modulated else "rmsnorm",
    )(*args)


def _mm_rope_kernel(a_ref, w_ref, cos_ref, sa_ref, sb_ref, o_ref, *, n_q_tiles):
    acc = _dot(a_ref[...], w_ref[...])
    cos, sa, sb = cos_ref[...], sa_ref[...], sb_ref[...]
    scale = jnp.where(pl.program_id(1) < n_q_tiles, Q_PRESCALE, 1.0).astype(F32)
    for c in range(acc.shape[1] // LANES):
        xc = acc[:, c * LANES:(c + 1) * LANES]
        up = pltpu.roll(xc, LANES - 16, 1)
        down = pltpu.roll(xc, 16, 1)
        r = xc * cos + up * sa + down * sb
        o_ref[:, c * LANES:(c + 1) * LANES] = (r * scale).astype(o_ref.dtype)


def _in_projection_qk(h, w, cos, sin_up, sin_down, *, rows_per_batch):
    m, k = h.shape
    n = w.shape[1]
    tm = min(1024, rows_per_batch)
    tn = 1024
    tpb = rows_per_batch // tm
    table = pl.BlockSpec((tm, LANES), lambda i, j: (i % tpb, 0))
    return pl.pallas_call(
        functools.partial(_mm_rope_kernel, n_q_tiles=n // (2 * tn)),
        grid=(m // tm, n // tn),
        in_specs=[
            pl.BlockSpec((tm, k), lambda i, j: (i, 0)),
            pl.BlockSpec((k, tn), lambda i, j: (0, j)),
            table, table, table,
        ],
        out_specs=pl.BlockSpec((tm, tn), lambda i, j: (i, j)),
        out_shape=jax.ShapeDtypeStruct((m, n), BF16),
        compiler_params=_cparams(2),
        name="in_proj_qk",
    )(h, w, cos, sin_up, sin_down)


def _mm_kernel(a_ref, w_ref, o_ref):
    o_ref[...] = _dot(a_ref[...], w_ref[...]).astype(o_ref.dtype)


def _in_projection_plain(h, w):
    m, k = h.shape
    n = w.shape[1]
    tm = min(1024, m)
    tn = 1024
    return pl.pallas_call(
        _mm_kernel,
        grid=(m // tm, n // tn),
        in_specs=[pl.BlockSpec((tm, k), lambda i, j: (i, 0)), pl.BlockSpec((k, tn), lambda i, j: (0, j))],
        out_specs=pl.BlockSpec((tm, tn), lambda i, j: (i, j)),
        out_shape=jax.ShapeDtypeStruct((m, n), BF16),
        compiler_params=_cparams(2),
        name="in_proj_plain",
    )(h, w)


def _mm_groups_kernel(a_ref, w_ref, o_ref):
    acc = _dot(a_ref[...], w_ref[...])
    for g in range(o_ref.shape[0]):
        o_ref[g] = acc[:, g * o_ref.shape[2]:(g + 1) * o_ref.shape[2]].astype(o_ref.dtype)


def _in_projection_fourier(h, w, *, batch):
    m, k = h.shape
    n = w.shape[1]
    seq = m // batch
    tm = min(1024, seq)
    tpb = seq // tm
    return pl.pallas_call(
        _mm_groups_kernel,
        grid=(m // tm,),
        in_specs=[pl.BlockSpec((tm, k), lambda i: (i, 0)), _resident((k, n), lambda i: (0, 0))],
        out_specs=pl.BlockSpec((None, FOURIER_GROUPS, tm, FOURIER_CH), lambda i: (i // tpb, 0, i % tpb, 0)),
        out_shape=jax.ShapeDtypeStruct((batch, FOURIER_GROUPS, seq, FOURIER_CH), BF16),
        compiler_params=_cparams(1),
        name="in_proj_fourier",
    )(h, w)


def _mm_relu2_kernel(a_ref, w_ref, o_ref):
    acc = jnp.maximum(_dot(a_ref[...], w_ref[...]), 0.0)
    o_ref[...] = (acc * acc).astype(o_ref.dtype)


def _matmul_relu2(a, w):
    m, k = a.shape
    n = w.shape[1]
    tm = min(1024, m)
    tn = 1024
    return pl.pallas_call(
        _mm_relu2_kernel,
        grid=(m // tm, n // tn),
        in_specs=[pl.BlockSpec((tm, k), lambda i, j: (i, 0)), pl.BlockSpec((k, tn), lambda i, j: (0, j))],
        out_specs=pl.BlockSpec((tm, tn), lambda i, j: (i, j)),
        out_shape=jax.ShapeDtypeStruct((m, n), BF16),
        compiler_params=_cparams(2),
        name="mlp_in_relu2",
    )(a, w)


def _mm_resid_kernel(a_ref, w_ref, x_ref, gate_ref, o_ref):
    o_ref[...] = x_ref[...] + gate_ref[...] * _dot(a_ref[...], w_ref[...])


def _matmul_gated_residual(a, w, x2d, gate, *, rows_per_batch, name):
    m, k = a.shape
    n = w.shape[1]
    tm = min(1024, rows_per_batch)
    tn = 256
    tpb = rows_per_batch // tm
    return pl.pallas_call(
        _mm_resid_kernel,
        grid=(m // tm, n // tn),
        in_specs=[
            pl.BlockSpec((tm, k), lambda i, j: (i, 0)),
            pl.BlockSpec((k, tn), lambda i, j: (0, j)),
            pl.BlockSpec((tm, tn), lambda i, j: (i, j)),
            pl.BlockSpec((None, 1, tn), lambda i, j: (i // tpb, 0, j)),
        ],
        out_specs=pl.BlockSpec((tm, tn), lambda i, j: (i, j)),
        out_shape=jax.ShapeDtypeStruct((m, n), F32),
        compiler_params=_cparams(2),
        name=name,
    )(a, w, x2d, gate)


def _attn_kernel(lam_ref, g_ref, q_ref, *refs, key_lens, tk, tq, lambda_init):
    n_src = len(key_lens)
    k_refs = refs[0:2 * n_src:2]
    v_refs = refs[1:2 * n_src:2]
    o_ref = refs[2 * n_src]
    k_all, vt_all, rhs_ref, s_ref, p_ref, mt_ref, alpha_ref, m_ref, acc_ref = refs[2 * n_src + 1:]
    q_blocks = q_ref.shape[0] // tq
    n_keys = sum(key_lens)
    n_tiles = n_keys // tk
    chunk = 256

    @pl.when(pl.program_id(2) == 0)
    def _():
        base = 0
        for k_ref, v_ref, klen in zip(k_refs, v_refs, key_lens):
            def cp(c, _, k_ref=k_ref, v_ref=v_ref, base=base):
                off = pl.multiple_of(c * chunk, chunk)
                k_all[pl.ds(base + off, chunk), :] = k_ref[pl.ds(off, chunk), :]
                vt_all[0:ATT_VD, pl.ds(base + off, chunk)] = (
                    v_ref[pl.ds(off, chunk), :].astype(F32).T.astype(BF16))
                return 0
            lax.fori_loop(0, klen // chunk, cp, 0)
            base += klen
        vt_all[ATT_VD:, :] = jnp.ones((vt_all.shape[0] - ATT_VD, n_keys), BF16)

    for u in range(q_blocks):
        qt = q_ref[u * tq:(u + 1) * tq, :].astype(F32).T
        row = lax.broadcasted_iota(jnp.int32, qt.shape, 0)
        rhs_ref[u, 0] = jnp.where(row < ATT_HD, qt, 0.0).astype(BF16)
        rhs_ref[u, 1] = jnp.where(row >= ATT_HD, qt, 0.0).astype(BF16)
    m_ref[...] = jnp.full(m_ref.shape, NEG_BIG, F32)
    stat_shape = m_ref.shape[2:]
    acc_ref[...] = jnp.zeros(acc_ref.shape, F32)

    def scores(g):
        u, t = divmod(g, n_tiles)
        par = g % 2
        k_tile = k_all[t * tk:(t + 1) * tk, :]
        for mp in range(2):
            s = _dot(k_tile, rhs_ref[u, mp])
            s_ref[par, mp] = s
            slab = jnp.max(s.reshape(tk // MAX_SLAB, MAX_SLAB, tq), axis=0)
            mt_ref[par, mp] = jnp.broadcast_to(jnp.max(slab, axis=0, keepdims=True), stat_shape)

    def softmax(g):
        u = g // n_tiles
        par = g % 2
        for mp in range(2):
            m_old = m_ref[u, mp]
            m_new = jnp.maximum(m_old, mt_ref[par, mp])
            alpha_ref[par, mp] = jnp.exp2(m_old - m_new)
            m_ref[u, mp] = m_new
            s = s_ref[par, mp].reshape(tk // F32_SUBLANES, F32_SUBLANES, tq)
            p_ref[par, mp] = jnp.exp2(s - m_new[None]).reshape(tk, tq).astype(BF16)

    def weighted_values(g):
        u, t = divmod(g, n_tiles)
        par = g % 2
        vt_tile = vt_all[:, t * tk:(t + 1) * tk]
        for mp in range(2):
            acc = acc_ref[u, mp].reshape(-1, F32_SUBLANES, tq) * alpha_ref[par, mp][None]
            acc_ref[u, mp] = acc.reshape(acc_ref.shape[2:]) + _dot(vt_tile, p_ref[par, mp])
        if t == n_tiles - 1:
            finalize(u)

    def finalize(u):
        lv = lam_ref[...]
        lam = (jnp.exp(jnp.sum(lv[0:1] * lv[1:2], axis=-1, keepdims=True))
               - jnp.exp(jnp.sum(lv[2:3] * lv[3:4], axis=-1, keepdims=True)) + lambda_init)
        a1, a2 = acc_ref[u, 0], acc_ref[u, 1]
        o = (a1[:ATT_VD] * (1.0 / a1[ATT_VD:ATT_VD + 1])
             - lam * (a2[:ATT_VD] * (1.0 / a2[ATT_VD:ATT_VD + 1])))
        o = o * lax.rsqrt(jnp.mean(o * o, axis=0, keepdims=True) + SUBLN_EPS)
        o_ref[u * tq:(u + 1) * tq, :] = (o.T * g_ref[...] * (1.0 - lambda_init)).astype(o_ref.dtype)

    total = q_blocks * n_tiles
    scores(0)
    if total > 1:
        scores(1)
    softmax(0)
    for g in range(total):
        if g + 2 < total:
            scores(g + 2)
        if g + 1 < total:
            softmax(g + 1)
        weighted_values(g)


def _key_tile(n_keys, cap):
    best = 256
    for t in range(256, cap + 1, 256):
        if n_keys % t == 0:
            best = t
    return best


def _diff_attention(lam_rows, subln_g, q_src, kv_srcs, *, batch, q_len, key_lens, lambda_init, v_block):
    h = ATT_HEADS
    tq = min(256, q_len)
    q_blocks = 2 if (q_len // tq) % 2 == 0 else 1
    n_keys = sum(key_lens)
    tk = _key_tile(n_keys, 256)
    nq = q_len // (tq * q_blocks)
    in_specs = [
        pl.BlockSpec((F32_SUBLANES, LANES), lambda b, hd, i: (0, 0)),
        pl.BlockSpec((1, ATT_VD), lambda b, hd, i: (0, 0)),
        pl.BlockSpec((q_blocks * tq, LANES), lambda b, hd, i: (b * nq + i, hd)),
    ]
    args = [lam_rows, subln_g.reshape(1, ATT_VD), q_src]
    for (k_src, v_src), klen in zip(kv_srcs, key_lens):
        in_specs += [pl.BlockSpec((klen, LANES), lambda b, hd, i: (b, h + hd)),
                     pl.BlockSpec((klen, LANES), lambda b, hd, i: (b, v_block + hd))]
        args += [k_src, v_src]
    stat = (F32_SUBLANES, tq)
    v_rows = ATT_VD + BF16_SUBLANES
    scratch = [pltpu.VMEM((n_keys, LANES), BF16), pltpu.VMEM((v_rows, n_keys), BF16),
               pltpu.VMEM((q_blocks, 2, LANES, tq), BF16), pltpu.VMEM((2, 2, tk, tq), F32),
               pltpu.VMEM((2, 2, tk, tq), BF16), pltpu.VMEM((2, 2) + stat, F32),
               pltpu.VMEM((2, 2) + stat, F32), pltpu.VMEM((q_blocks, 2) + stat, F32),
               pltpu.VMEM((q_blocks, 2, v_rows, tq), F32)]
    return pl.pallas_call(
        functools.partial(_attn_kernel, key_lens=tuple(key_lens), tk=tk, tq=tq, lambda_init=lambda_init),
        grid=(batch, h, nq),
        in_specs=in_specs,
        out_specs=pl.BlockSpec((q_blocks * tq, ATT_VD), lambda b, hd, i: (b * nq + i, hd)),
        out_shape=jax.ShapeDtypeStruct((batch * q_len, h * ATT_VD), BF16),
        scratch_shapes=scratch,
        compiler_params=_cparams(3),
        name="diff_attention",
    )(*args)


def _gelu(x):
    return 0.5 * x * (1.0 + lax.erf(x * (2.0 ** -0.5)))


def _sgu_kernel(su_ref, sv_ref, w_ref, b_ref, g_ref, o_ref):
    for c in range(su_ref.shape[0] // SGU_CHUNK):
        rows = slice(c * SGU_CHUNK, (c + 1) * SGU_CHUNK)
        u = _gelu(su_ref[rows, :].astype(F32))
        v = _gelu(sv_ref[rows, :].astype(F32))
        v = (v * lax.rsqrt(jnp.mean(v * v, axis=-1, keepdims=True) + EPS) * g_ref[...]).astype(BF16)
        for g in range(SGU_GROUPS):
            cols = slice(g * SGU_CH, (g + 1) * SGU_CH)
            mixed = _dot(w_ref[g], v[:, cols]) + b_ref[g]
            o_ref[rows, cols] = (u[:, cols] * mixed).astype(o_ref.dtype)


def _spatial_gating(p, w_s, b_s_lanes, g_v, *, su_block, rows_per_batch):
    m = p.shape[0]
    width = SGU_GROUPS * SGU_CH
    tm = min(512, rows_per_batch)
    return pl.pallas_call(
        _sgu_kernel,
        grid=(m // tm,),
        in_specs=[
            pl.BlockSpec((tm, width), lambda i: (i, su_block)),
            pl.BlockSpec((tm, width), lambda i: (i, su_block + 1)),
            pl.BlockSpec(w_s.shape, lambda i: (0, 0, 0)),
            pl.BlockSpec(b_s_lanes.shape, lambda i: (0, 0, 0)),
            pl.BlockSpec((1, width), lambda i: (0, 0)),
        ],
        out_specs=pl.BlockSpec((tm, width), lambda i: (i, 0)),
        out_shape=jax.ShapeDtypeStruct((m, width), BF16),
        compiler_params=_cparams(1),
        name="spatial_gating",
    )(p, p, w_s, b_s_lanes, g_v.reshape(1, width))


def _dft_cos_sin(n, scale):
    j = jnp.arange(n, dtype=jnp.int32)
    ang = ((j[:, None] * j[None, :]) % n).astype(F32) * (2.0 * math.pi / n)
    return jnp.cos(ang) * scale, jnp.sin(ang) * scale


def _fourier_stage1_kernel(x_ref, w_ref, tc_ref, ts_ref, o_ref):
    n2 = x_ref.shape[0]
    y = _dot(w_ref[...], x_ref[...])
    br, bi = y[:n2], y[n2:]
    reps = FOURIER_CH // LANES

    def widen(t):
        parts = [t[:, q * LANES:(q + 1) * LANES] for q in range(t.shape[1] // LANES)]
        return jnp.concatenate([p for p in parts for _ in range(reps)], axis=1)

    tc, ts = widen(tc_ref[...]), widen(ts_ref[...])
    o_ref[0] = (br * tc + bi * ts).astype(o_ref.dtype)
    o_ref[1] = (bi * tc - br * ts).astype(o_ref.dtype)


def _fourier_stage2_kernel(b_ref, w3_ref, cc_ref, sc_ref, o_ref):
    _, r, n1, ch = b_ref.shape
    w3, cc, sc = w3_ref[...], cc_ref[...], sc_ref[...]
    for j in range(r):
        bj = jnp.concatenate([b_ref[0, j], b_ref[1, j]], axis=0)
        y = _dot(w3, bj)
        out = _dot(y[:n1].astype(BF16), cc) + _dot(y[n1:].astype(BF16), sc)
        o_ref[:, j, :] = out


def _fourier_long(pf, *, batch, seq):
    n1 = LANES
    n2 = seq // n1
    ch = FOURIER_CH
    g_n = FOURIER_GROUPS
    r1 = min(16, n1)
    r2 = F32_SUBLANES

    c2, s2 = _dft_cos_sin(n2, n2 ** -0.5)
    w1 = jnp.concatenate([c2, -s2], axis=0).astype(BF16)

    kk = jnp.arange(n2, dtype=jnp.int32)[:, None] * jnp.arange(n1, dtype=jnp.int32)[None, :]
    ang = (kk % seq).astype(F32) * (2.0 * math.pi / seq)
    tw_c = jnp.broadcast_to(jnp.cos(ang)[:, :, None], (n2, n1, LANES)).reshape(n2, n1 * LANES)
    tw_s = jnp.broadcast_to(jnp.sin(ang)[:, :, None], (n2, n1, LANES)).reshape(n2, n1 * LANES)

    c1, s1 = _dft_cos_sin(n1, n1 ** -0.5)
    k3 = jnp.concatenate([jnp.concatenate([c1, s1], axis=1),
                          jnp.concatenate([-s1, c1], axis=1)], axis=0).astype(BF16)
    cc, sc = _dft_cos_sin(ch, ch ** -0.5)
    cc, sc = cc.astype(BF16), sc.astype(BF16)

    x2 = pf.reshape(batch, g_n, n2, n1 * ch)
    stage1 = pl.pallas_call(
        _fourier_stage1_kernel,
        grid=(batch, g_n, n1 // r1),
        in_specs=[
            pl.BlockSpec((None, None, n2, r1 * ch), lambda b, g, i: (b, g, 0, i)),
            _resident(w1.shape, lambda b, g, i: (0, 0)),
            pl.BlockSpec((n2, r1 * LANES), lambda b, g, i: (0, i)),
            pl.BlockSpec((n2, r1 * LANES), lambda b, g, i: (0, i)),
        ],
        out_specs=pl.BlockSpec((None, None, 2, n2, r1 * ch), lambda b, g, i: (b, g, 0, 0, i)),
        out_shape=jax.ShapeDtypeStruct((batch, g_n, 2, n2, n1 * ch), BF16),
        compiler_params=_cparams(3),
        name="fourier_stage1",
    )(x2, w1, tw_c, tw_s)
    stage1 = stage1.reshape(batch, g_n, 2, n2, n1, ch)

    out = pl.pallas_call(
        _fourier_stage2_kernel,
        grid=(batch, g_n, n2 // r2),
        in_specs=[
            pl.BlockSpec((None, None, 2, r2, n1, ch), lambda b, g, i: (b, g, 0, i, 0, 0)),
            _resident(k3.shape, lambda b, g, i: (0, 0)),
            _resident(cc.shape, lambda b, g, i: (0, 0)),
            _resident(sc.shape, lambda b, g, i: (0, 0)),
        ],
        out_specs=pl.BlockSpec((None, n1, r2, ch), lambda b, g, i: (b, 0, i, g)),
        out_shape=jax.ShapeDtypeStruct((batch, n1, n2, g_n * ch), F32),
        compiler_params=_cparams(3),
        name="fourier_stage2",
    )(stage1, k3, cc, sc)
    return out.reshape(batch * seq, g_n * ch)


def _fourier_dense_kernel(x_ref, cl_ref, sl_ref, cc_ref, sc_ref, o_ref):
    x = x_ref[...]
    a = _dot(cl_ref[...], x).astype(BF16)
    b = _dot(sl_ref[...], x).astype(BF16)
    o_ref[...] = _dot(a, cc_ref[...]) - _dot(b, sc_ref[...])


def _fourier_short(pf, *, batch, seq):
    ch = FOURIER_CH
    g_n = FOURIER_GROUPS
    cl, sl = _dft_cos_sin(seq, seq ** -0.5)
    cc, sc = _dft_cos_sin(ch, ch ** -0.5)
    const = lambda a: _resident(a.shape, lambda b, g: (0, 0))
    mats = [m.astype(BF16) for m in (cl, sl, cc, sc)]
    return pl.pallas_call(
        _fourier_dense_kernel,
        grid=(batch, g_n),
        in_specs=[pl.BlockSpec((None, None, seq, ch), lambda b, g: (b, g, 0, 0))] + [const(m) for m in mats],
        out_specs=pl.BlockSpec((seq, ch), lambda b, g: (b, g)),
        out_shape=jax.ShapeDtypeStruct((batch * seq, g_n * ch), F32),
        compiler_params=_cparams(2),
        name="fourier_dense",
    )(pf, *mats)


def _merge_kernel(att_ref, sgu_ref, four_ref, ga_ref, gg_ref, gf_ref, wa_ref, ws_ref, wf_ref, wo_ref,
                  x_ref, gate_ref, g_ref, sh_ref, sc_ref, xo_ref, ho_ref):
    y = jax.nn.sigmoid(ga_ref[...].astype(F32)) * _dot(att_ref[...], wa_ref[...])
    y = y + jax.nn.sigmoid(gg_ref[...].astype(F32)) * _dot(sgu_ref[...], ws_ref[...])
    y = y + jax.nn.sigmoid(gf_ref[...].astype(F32)) * _dot(four_ref[...].astype(BF16), wf_ref[...])
    x = x_ref[...] + gate_ref[...] * _dot(y.astype(BF16), wo_ref[...])
    xo_ref[...] = x
    h = x * lax.rsqrt(jnp.mean(x * x, axis=-1, keepdims=True) + EPS) * g_ref[...]
    ho_ref[...] = (h * (1.0 + sc_ref[...]) + sh_ref[...]).astype(ho_ref.dtype)


def _merge_project_norm(att, sgu, four, p, w_pa, w_ps, w_pf, w_o, x2d, gate, g, shift, scale, *,
                        gate_block, rows_per_batch):
    m, k = att.shape
    d = w_pa.shape[1]
    tm = min(256, rows_per_batch)
    tpb = rows_per_batch // tm
    act = pl.BlockSpec((tm, k), lambda i: (i, 0))
    gcol = lambda off: pl.BlockSpec((tm, d), lambda i: (i, gate_block + off))
    row = pl.BlockSpec((tm, d), lambda i: (i, 0))
    wspec = _resident((k, d), lambda i: (0, 0))
    vec = pl.BlockSpec((None, 1, d), lambda i: (i // tpb, 0, 0))
    return pl.pallas_call(
        _merge_kernel,
        grid=(m // tm,),
        in_specs=[act, act, act, gcol(0), gcol(1), gcol(2), wspec, wspec, wspec,
                  _resident((d, d), lambda i: (0, 0)), row, vec,
                  pl.BlockSpec((1, d), lambda i: (0, 0)), vec, vec],
        out_specs=[row, row],
        out_shape=[jax.ShapeDtypeStruct((m, d), F32), jax.ShapeDtypeStruct((m, d), BF16)],
        compiler_params=_cparams(1),
        name="merge_outproj_norm",
    )(att, sgu, four, p, p, p, w_pa, w_ps, w_pf, w_o, x2d, gate, g.reshape(1, d), shift, scale)


def _rope_tables(n_tokens):
    n_rows = n_tokens // GRID_W
    row = jnp.broadcast_to(jnp.arange(n_rows, dtype=F32)[:, None], (n_rows, GRID_W)).reshape(-1)
    col = jnp.broadcast_to(jnp.arange(GRID_W, dtype=F32)[None, :], (n_rows, GRID_W)).reshape(-1)
    n_freq = ATT_HD // 4
    inv = ROPE_BASE ** (-jnp.arange(n_freq, dtype=F32) / n_freq)
    ar = row[:, None] * inv
    ac = col[:, None] * inv
    ang = jnp.concatenate([ar, ar, ac, ac] * (LANES // ATT_HD), axis=-1)
    cos, sin = jnp.cos(ang), jnp.sin(ang)
    first_half = (jnp.arange(LANES) % (ATT_HD // 2)) < (ATT_HD // 4)
    return cos, jnp.where(first_half, -sin, 0.0), jnp.where(first_half, 0.0, sin)


def _split_in_projection(w_in):
    d = w_in.shape[0]
    qk = ATT_HEADS * ATT_HD
    v0 = 4 * qk
    su0 = v0 + ATT_HEADS * ATT_VD
    f0 = su0 + 2 * SGU_GROUPS * SGU_CH
    g0 = f0 + FOURIER_GROUPS * FOURIER_CH
    seg = lambda s: w_in[:, s * qk:(s + 1) * qk].astype(BF16).reshape(d, ATT_HEADS, ATT_HD)
    pair = lambda a, b: jnp.concatenate([seg(a), seg(b)], axis=-1).reshape(d, 2 * qk)
    w_qk = jnp.concatenate([pair(0, 1), pair(2, 3)], axis=-1)
    w_rest = jnp.concatenate([w_in[:, g0:].astype(BF16), w_in[:, v0:f0].astype(BF16)], axis=-1)
    return w_qk, w_rest, w_in[:, f0:g0].astype(BF16)


def kernel(x, c, ctx, c_ctx, w_mod, b_mod, norm1_g, norm2_g, w_in, lambda_q1, lambda_k1, lambda_q2, lambda_k2,
           subln_g, sgu_norm_g, sgu_w, sgu_b, w_proj_att, w_proj_sgu, w_proj_fourier, w_out, w_mlp_in, w_mlp_out,
           final_g):
    batch, seq, d = x.shape
    ctx_len = ctx.shape[1]
    depth = w_mod.shape[0]
    in_width = w_in.shape[2]
    qk_width = 2 * ATT_HEADS * ATT_HD
    v_width = ATT_HEADS * ATT_VD
    sgu_width = SGU_GROUPS * SGU_CH
    four_width = FOURIER_GROUPS * FOURIER_CH
    assert in_width == 2 * qk_width + v_width + 2 * sgu_width + four_width + 3 * d
    assert seq % (LANES * F32_SUBLANES) == 0 and seq % GRID_W == 0 and ctx_len % 256 == 0
    gate_block = 0
    v_block = 3 * d // LANES
    su_block = (3 * d + v_width) // sgu_width
    assert (3 * d + v_width) % sgu_width == 0

    sgu_b_lanes = jnp.broadcast_to(sgu_b[..., None], sgu_b.shape + (SGU_CH,))

    cos, sin_up, sin_down = _rope_tables(seq)
    ctx_rows = min(1024, ctx_len)
    one_t, zero_t = jnp.ones((ctx_rows, LANES), F32), jnp.zeros((ctx_rows, LANES), F32)

    pad = jnp.zeros((F32_SUBLANES - batch - 1, d), F32)
    c_rows = jnp.concatenate([c, c_ctx[None, :], pad], axis=0)
    mod = _modulation(c_rows, w_mod, b_mod)

    xs = x.reshape(batch * seq, d)
    cs = ctx.reshape(batch * ctx_len, d)
    for l in range(depth):
        last = l == depth - 1
        lambda_init = 0.8 - 0.6 * math.exp(-0.3 * l)
        lam_rows = jnp.stack([lambda_q1[l], lambda_k1[l], lambda_q2[l], lambda_k2[l]]).astype(F32)
        lam_rows = jnp.pad(lam_rows, ((0, F32_SUBLANES - 4), (0, LANES - ATT_HD)))
        mod_x = mod[l, :batch].reshape(batch, 1, N_MOD, d)
        mod_c = mod[l, batch:batch + 1].reshape(1, 1, N_MOD, d)
        mx = [mod_x[:, :, i] for i in range(N_MOD)]
        mc = [mod_c[:, :, i] for i in range(N_MOD)]

        w_qk, w_rest, w_four = _split_in_projection(w_in[l])
        w_pa, w_ps, w_pf = (w[l].astype(BF16) for w in (w_proj_att, w_proj_sgu, w_proj_fourier))
        w_o, w_1, w_2 = (w[l].astype(BF16) for w in (w_out, w_mlp_in, w_mlp_out))
        sgu_w_b = sgu_w[l].astype(BF16)

        hx = _rmsnorm(xs, norm1_g[l], mx[0], mx[1], rows_per_batch=seq, out_dtype=BF16)
        hc = _rmsnorm(cs, norm1_g[l], mc[0], mc[1], rows_per_batch=batch * ctx_len, out_dtype=BF16)
        qk_x = _in_projection_qk(hx, w_qk, cos, sin_up, sin_down, rows_per_batch=seq)
        qk_c = _in_projection_qk(hc, w_qk, one_t, zero_t, zero_t, rows_per_batch=ctx_len)
        px = _in_projection_plain(hx, w_rest)
        pc = _in_projection_plain(hc, w_rest)
        fx = _in_projection_fourier(hx, w_four, batch=batch)

        att_x = _diff_attention(lam_rows, subln_g[l], qk_x, [(qk_x, px), (qk_c, pc)], batch=batch, q_len=seq,
                                key_lens=[seq, ctx_len], lambda_init=lambda_init, v_block=v_block)
        sgu_x = _spatial_gating(px, sgu_w_b, sgu_b_lanes[l], sgu_norm_g[l], su_block=su_block,
                                rows_per_batch=seq)
        four_x = _fourier_long(fx, batch=batch, seq=seq)

        if not last:
            fc = _in_projection_fourier(hc, w_four, batch=batch)
            att_c = _diff_attention(lam_rows, subln_g[l], qk_c, [(qk_c, pc)], batch=batch, q_len=ctx_len,
                                    key_lens=[ctx_len], lambda_init=lambda_init, v_block=v_block)
            sgu_c = _spatial_gating(pc, sgu_w_b, sgu_b_lanes[l], sgu_norm_g[l], su_block=su_block,
                                    rows_per_batch=ctx_len)
            four_c = _fourier_short(fc, batch=batch, seq=ctx_len)
            cs, hc2 = _merge_project_norm(att_c, sgu_c, four_c, pc, w_pa, w_ps, w_pf, w_o, cs,
                                          mc[2], norm2_g[l], mc[3], mc[4], gate_block=gate_block,
                                          rows_per_batch=batch * ctx_len)
            cs = _matmul_gated_residual(_matmul_relu2(hc2, w_1), w_2, cs, mc[5],
                                        rows_per_batch=batch * ctx_len, name="mlp_out_residual")

        xs, hx2 = _merge_project_norm(att_x, sgu_x, four_x, px, w_pa, w_ps, w_pf, w_o, xs,
                                      mx[2], norm2_g[l], mx[3], mx[4], gate_block=gate_block,
                                      rows_per_batch=seq)
        xs = _matmul_gated_residual(_matmul_relu2(hx2, w_1), w_2, xs, mx[5], rows_per_batch=seq,
                                    name="mlp_out_residual")

    out = _rmsnorm(xs, final_g, rows_per_batch=seq, out_dtype=x.dtype)
    return out.reshape(batch, seq, d)
```

```python
import functools
import math

import jax
import jax.numpy as jnp
from jax import lax
from jax.experimental import pallas as pl
from jax.experimental.pallas import tpu as pltpu

F32 = jnp.float32
BF16 = jnp.bfloat16

GRID_W = 64
ATT_HEADS = 8
ATT_HD = 64
ATT_VD = 2 * ATT_HD
ROPE_BASE = 10000.0
SGU_GROUPS = 8
SGU_CH = 128
SGU_CHUNK = 128
FOURIER_GROUPS = 4
FOURIER_CH = 256
N_MOD = 6
EPS = 1e-6
SUBLN_EPS = 1e-5

LANES = 128
BF16_SUBLANES = 16
F32_SUBLANES = 8
VMEM_LIMIT_BYTES = 56 * 2**20

LOG2E = 1.4426950408889634
Q_PRESCALE = (ATT_HD ** -0.5) * LOG2E
NEG_BIG = -1e30
MAX_SLAB = 64


def _cparams(n_axes):
    return pltpu.CompilerParams(dimension_semantics=("arbitrary",) * n_axes,
                                vmem_limit_bytes=VMEM_LIMIT_BYTES)


def _dot(a, b):
    return jnp.dot(a, b, preferred_element_type=F32)


def _resident(shape, index_map):
    return pl.BlockSpec(shape, index_map, pipeline_mode=pl.Buffered(1))


def _mod_kernel(c_ref, w_ref, b_ref, o_ref):
    a = c_ref[...]
    a = a * jax.nn.sigmoid(a)
    w = w_ref[...]
    a_hi = a.astype(BF16)
    a_lo = (a - a_hi.astype(F32)).astype(BF16)
    w_hi = w.astype(BF16)
    w_lo = (w - w_hi.astype(F32)).astype(BF16)
    o_ref[...] = _dot(a_hi, w_hi) + _dot(a_hi, w_lo) + _dot(a_lo, w_hi) + b_ref[...]


def _modulation(c_rows, w_mod, b_mod):
    depth, d, n = w_mod.shape
    tn = 512
    rows = c_rows.shape[0]
    return pl.pallas_call(
        _mod_kernel,
        grid=(depth, n // tn),
        in_specs=[
            pl.BlockSpec((rows, d), lambda l, j: (0, 0)),
            pl.BlockSpec((None, d, tn), lambda l, j: (l, 0, j)),
            pl.BlockSpec((None, 1, tn), lambda l, j: (l, 0, j)),
        ],
        out_specs=pl.BlockSpec((None, rows, tn), lambda l, j: (l, 0, j)),
        out_shape=jax.ShapeDtypeStruct((depth, rows, n), F32),
        compiler_params=_cparams(2),
        name="adaln_mod",
    )(c_rows, w_mod, b_mod.reshape(depth, 1, n))


def _norm_kernel(x_ref, g_ref, *rest, eps, modulated):
    x = x_ref[...]
    y = x * lax.rsqrt(jnp.mean(x * x, axis=-1, keepdims=True) + eps) * g_ref[...]
    if modulated:
        sh_ref, sc_ref, o_ref = rest
        y = y * (1.0 + sc_ref[...]) + sh_ref[...]
    else:
        (o_ref,) = rest
    o_ref[...] = y.astype(o_ref.dtype)


def _rmsnorm(x2d, g, shift=None, scale=None, *, rows_per_batch, out_dtype):
    m, d = x2d.shape
    tm = min(512, rows_per_batch)
    tpb = rows_per_batch // tm
    modulated = shift is not None
    in_specs = [pl.BlockSpec((tm, d), lambda i: (i, 0)), pl.BlockSpec((1, d), lambda i: (0, 0))]
    args = [x2d, g.reshape(1, d)]
    if modulated:
        vec = pl.BlockSpec((None, 1, d), lambda i: (i // tpb, 0, 0))
        in_specs += [vec, vec]
        args += [shift, scale]
    return pl.pallas_call(
        functools.partial(_norm_kernel, eps=EPS, modulated=modulated),
        grid=(m // tm,),
        in_specs=in_specs,
        out_specs=pl.BlockSpec((tm, d), lambda i: (i, 0)),
        out_shape=jax.ShapeDtypeStruct((m, d), out_dtype),
        compiler_params=_cparams(1),
        name="rmsnorm_mod" if modulated else "rmsnorm",
    )(*args)


def _mm_rope_kernel(a_ref, w_ref, cos_ref, sa_ref, sb_ref, o_ref, *, n_q_tiles):
    acc = _dot(a_ref[...], w_ref[...])
    cos, sa, sb = cos_ref[...], sa_ref[...], sb_ref[...]
    scale = jnp.where(pl.program_id(1) < n_q_tiles, Q_PRESCALE, 1.0).astype(F32)
    for c in range(acc.shape[1] // LANES):
        xc = acc[:, c * LANES:(c + 1) * LANES]
        up = pltpu.roll(xc, LANES - 16, 1)
        down = pltpu.roll(xc, 16, 1)
        r = xc * cos + up * sa + down * sb
        o_ref[:, c * LANES:(c + 1) * LANES] = (r * scale).astype(o_ref.dtype)


def _in_projection_qk(h, w, cos, sin_up, sin_down, *, rows_per_batch):
    m, k = h.shape
    n = w.shape[1]
    tm = min(1024, rows_per_batch)
    tn = 1024
    tpb = rows_per_batch // tm
    table = pl.BlockSpec((tm, LANES), lambda i, j: (i % tpb, 0))
    return pl.pallas_call(
        functools.partial(_mm_rope_kernel, n_q_tiles=n // (2 * tn)),
        grid=(m // tm, n // tn),
        in_specs=[
            pl.BlockSpec((tm, k), lambda i, j: (i, 0)),
            pl.BlockSpec((k, tn), lambda i, j: (0, j)),
            table, table, table,
        ],
        out_specs=pl.BlockSpec((tm, tn), lambda i, j: (i, j)),
        out_shape=jax.ShapeDtypeStruct((m, n), BF16),
        compiler_params=_cparams(2),
        name="in_proj_qk",
    )(h, w, cos, sin_up, sin_down)


def _mm_kernel(a_ref, w_ref, o_ref):
    o_ref[...] = _dot(a_ref[...], w_ref[...]).astype(o_ref.dtype)


def _in_projection_plain(h, w):
    m, k = h.shape
    n = w.shape[1]
    tm = min(1024, m)
    tn = 1024
    return pl.pallas_call(
        _mm_kernel,
        grid=(m // tm, n // tn),
        in_specs=[pl.BlockSpec((tm, k), lambda i, j: (i, 0)), pl.BlockSpec((k, tn), lambda i, j: (0, j))],
        out_specs=pl.BlockSpec((tm, tn), lambda i, j: (i, j)),
        out_shape=jax.ShapeDtypeStruct((m, n), BF16),
        compiler_params=_cparams(2),
        name="in_proj_plain",
    )(h, w)


def _mm_groups_kernel(a_ref, w_ref, o_ref):
    acc = _dot(a_ref[...], w_ref[...])
    for g in range(o_ref.shape[0]):
        o_ref[g] = acc[:, g * o_ref.shape[2]:(g + 1) * o_ref.shape[2]].astype(o_ref.dtype)


def _in_projection_fourier(h, w, *, batch):
    m, k = h.shape
    n = w.shape[1]
    seq = m // batch
    tm = min(1024, seq)
    tpb = seq // tm
    return pl.pallas_call(
        _mm_groups_kernel,
        grid=(m // tm,),
        in_specs=[pl.BlockSpec((tm, k), lambda i: (i, 0)), _resident((k, n), lambda i: (0, 0))],
        out_specs=pl.BlockSpec((None, FOURIER_GROUPS, tm, FOURIER_CH), lambda i: (i // tpb, 0, i % tpb, 0)),
        out_shape=jax.ShapeDtypeStruct((batch, FOURIER_GROUPS, seq, FOURIER_CH), BF16),
        compiler_params=_cparams(1),
        name="in_proj_fourier",
    )(h, w)


def _mlp_kernel(h_ref, w1_ref, w2_ref, x_ref, gate_ref, o_ref, acc_ref):
    c = pl.program_id(1)

    @pl.when(c == 0)
    def _():
        acc_ref[...] = jnp.zeros(acc_ref.shape, F32)

    hid = jnp.maximum(_dot(h_ref[...], w1_ref[...]), 0.0)
    acc_ref[...] += _dot((hid * hid).astype(BF16), w2_ref[...])

    @pl.when(c == pl.num_programs(1) - 1)
    def _():
        o_ref[...] = x_ref[...] + gate_ref[...] * acc_ref[...]


def _mlp_gated_residual(h, w1, w2, x2d, gate, *, rows_per_batch):
    m, d = h.shape
    hidden = w1.shape[1]
    tm = min(512, rows_per_batch)
    tc = 1024
    tpb = rows_per_batch // tm
    row = lambda i, c: (i, 0)
    return pl.pallas_call(
        _mlp_kernel,
        grid=(m // tm, hidden // tc),
        in_specs=[
            pl.BlockSpec((tm, d), row),
            pl.BlockSpec((d, tc), lambda i, c: (0, c)),
            pl.BlockSpec((tc, d), lambda i, c: (c, 0)),
            pl.BlockSpec((tm, d), row),
            pl.BlockSpec((None, 1, d), lambda i, c: (i // tpb, 0, 0)),
        ],
        out_specs=pl.BlockSpec((tm, d), row),
        out_shape=jax.ShapeDtypeStruct((m, d), F32),
        scratch_shapes=[pltpu.VMEM((tm, d), F32)],
        compiler_params=_cparams(2),
        name="mlp_fused",
    )(h, w1, w2, x2d, gate)


def _attn_kernel(lam_ref, g_ref, q_ref, *refs, key_lens, tk, tq, lambda_init):
    n_src = len(key_lens)
    k_refs = refs[0:2 * n_src:2]
    v_refs = refs[1:2 * n_src:2]
    o_ref = refs[2 * n_src]
    k_all, vt_all, rhs_ref, s_ref, p_ref, mt_ref, alpha_ref, m_ref, acc_ref = refs[2 * n_src + 1:]
    q_blocks = q_ref.shape[0] // tq
    n_keys = sum(key_lens)
    n_tiles = n_keys // tk
    chunk = 256

    @pl.when(pl.program_id(2) == 0)
    def _():
        base = 0
        for k_ref, v_ref, klen in zip(k_refs, v_refs, key_lens):
            def cp(c, _, k_ref=k_ref, v_ref=v_ref, base=base):
                off = pl.multiple_of(c * chunk, chunk)
                k_all[pl.ds(base + off, chunk), :] = k_ref[pl.ds(off, chunk), :]
                vt_all[0:ATT_VD, pl.ds(base + off, chunk)] = (
                    v_ref[pl.ds(off, chunk), :].astype(F32).T.astype(BF16))
                return 0
            lax.fori_loop(0, klen // chunk, cp, 0)
            base += klen
        vt_all[ATT_VD:, :] = jnp.ones((vt_all.shape[0] - ATT_VD, n_keys), BF16)

    for u in range(q_blocks):
        qt = q_ref[u * tq:(u + 1) * tq, :].astype(F32).T
        row = lax.broadcasted_iota(jnp.int32, qt.shape, 0)
        rhs_ref[u, 0] = jnp.where(row < ATT_HD, qt, 0.0).astype(BF16)
        rhs_ref[u, 1] = jnp.where(row >= ATT_HD, qt, 0.0).astype(BF16)
    m_ref[...] = jnp.full(m_ref.shape, NEG_BIG, F32)
    stat_shape = m_ref.shape[2:]
    acc_ref[...] = jnp.zeros(acc_ref.shape, F32)

    def scores(g):
        u, t = divmod(g, n_tiles)
        par = g % 2
        k_tile = k_all[t * tk:(t + 1) * tk, :]
        for mp in range(2):
            s = _dot(k_tile, rhs_ref[u, mp])
            s_ref[par, mp] = s
            slab = jnp.max(s.reshape(tk // MAX_SLAB, MAX_SLAB, tq), axis=0)
            mt_ref[par, mp] = jnp.broadcast_to(jnp.max(slab, axis=0, keepdims=True), stat_shape)

    def softmax(g):
        u = g // n_tiles
        par = g % 2
        for mp in range(2):
            m_old = m_ref[u, mp]
            m_new = jnp.maximum(m_old, mt_ref[par, mp])
            alpha_ref[par, mp] = jnp.exp2(m_old - m_new)
            m_ref[u, mp] = m_new
            s = s_ref[par, mp].reshape(tk // F32_SUBLANES, F32_SUBLANES, tq)
            p_ref[par, mp] = jnp.exp2(s - m_new[None]).reshape(tk, tq).astype(BF16)

    def weighted_values(g):
        u, t = divmod(g, n_tiles)
        par = g % 2
        vt_tile = vt_all[:, t * tk:(t + 1) * tk]
        for mp in range(2):
            acc = acc_ref[u, mp].reshape(-1, F32_SUBLANES, tq) * alpha_ref[par, mp][None]
            acc_ref[u, mp] = acc.reshape(acc_ref.shape[2:]) + _dot(vt_tile, p_ref[par, mp])
        if t == n_tiles - 1:
            finalize(u)

    def finalize(u):
        lv = lam_ref[...]
        lam = (jnp.exp(jnp.sum(lv[0:1] * lv[1:2], axis=-1, keepdims=True))
               - jnp.exp(jnp.sum(lv[2:3] * lv[3:4], axis=-1, keepdims=True)) + lambda_init)
        a1, a2 = acc_ref[u, 0], acc_ref[u, 1]
        o = (a1[:ATT_VD] * (1.0 / a1[ATT_VD:ATT_VD + 1])
             - lam * (a2[:ATT_VD] * (1.0 / a2[ATT_VD:ATT_VD + 1])))
        o = o * lax.rsqrt(jnp.mean(o * o, axis=0, keepdims=True) + SUBLN_EPS)
        o_ref[u * tq:(u + 1) * tq, :] = (o.T * g_ref[...] * (1.0 - lambda_init)).astype(o_ref.dtype)

    total = q_blocks * n_tiles
    scores(0)
    if total > 1:
        scores(1)
    softmax(0)
    for g in range(total):
        if g + 2 < total:
            scores(g + 2)
        if g + 1 < total:
            softmax(g + 1)
        weighted_values(g)


def _key_tile(n_keys, cap):
    best = 256
    for t in range(256, cap + 1, 256):
        if n_keys % t == 0:
            best = t
    return best


def _diff_attention(lam_rows, subln_g, q_src, kv_srcs, *, batch, q_len, key_lens, lambda_init, v_block):
    h = ATT_HEADS
    tq = min(256, q_len)
    q_blocks = 2 if (q_len // tq) % 2 == 0 else 1
    n_keys = sum(key_lens)
    tk = _key_tile(n_keys, 256)
    nq = q_len // (tq * q_blocks)
    in_specs = [
        pl.BlockSpec((F32_SUBLANES, LANES), lambda b, hd, i: (0, 0)),
        pl.BlockSpec((1, ATT_VD), lambda b, hd, i: (0, 0)),
        pl.BlockSpec((q_blocks * tq, LANES), lambda b, hd, i: (b * nq + i, hd)),
    ]
    args = [lam_rows, subln_g.reshape(1, ATT_VD), q_src]
    for (k_src, v_src), klen in zip(kv_srcs, key_lens):
        in_specs += [pl.BlockSpec((klen, LANES), lambda b, hd, i: (b, h + hd)),
                     pl.BlockSpec((klen, LANES), lambda b, hd, i: (b, v_block + hd))]
        args += [k_src, v_src]
    stat = (F32_SUBLANES, tq)
    v_rows = ATT_VD + BF16_SUBLANES
    scratch = [pltpu.VMEM((n_keys, LANES), BF16), pltpu.VMEM((v_rows, n_keys), BF16),
               pltpu.VMEM((q_blocks, 2, LANES, tq), BF16), pltpu.VMEM((2, 2, tk, tq), F32),
               pltpu.VMEM((2, 2, tk, tq), BF16), pltpu.VMEM((2, 2) + stat, F32),
               pltpu.VMEM((2, 2) + stat, F32), pltpu.VMEM((q_blocks, 2) + stat, F32),
               pltpu.VMEM((q_blocks, 2, v_rows, tq), F32)]
    return pl.pallas_call(
        functools.partial(_attn_kernel, key_lens=tuple(key_lens), tk=tk, tq=tq, lambda_init=lambda_init),
        grid=(batch, h, nq),
        in_specs=in_specs,
        out_specs=pl.BlockSpec((q_blocks * tq, ATT_VD), lambda b, hd, i: (b * nq + i, hd)),
        out_shape=jax.ShapeDtypeStruct((batch * q_len, h * ATT_VD), BF16),
        scratch_shapes=scratch,
        compiler_params=_cparams(3),
        name="diff_attention",
    )(*args)


def _gelu(x):
    return 0.5 * x * (1.0 + lax.erf(x * (2.0 ** -0.5)))


def _sgu_kernel(su_ref, sv_ref, w_ref, b_ref, g_ref, o_ref):
    for c in range(su_ref.shape[0] // SGU_CHUNK):
        rows = slice(c * SGU_CHUNK, (c + 1) * SGU_CHUNK)
        u = _gelu(su_ref[rows, :].astype(F32))
        v = _gelu(sv_ref[rows, :].astype(F32))
        v = (v * lax.rsqrt(jnp.mean(v * v, axis=-1, keepdims=True) + EPS) * g_ref[...]).astype(BF16)
        for g in range(SGU_GROUPS):
            cols = slice(g * SGU_CH, (g + 1) * SGU_CH)
            mixed = _dot(w_ref[g], v[:, cols]) + b_ref[g]
            o_ref[rows, cols] = (u[:, cols] * mixed).astype(o_ref.dtype)


def _spatial_gating(p, w_s, b_s_lanes, g_v, *, su_block, rows_per_batch):
    m = p.shape[0]
    width = SGU_GROUPS * SGU_CH
    tm = min(512, rows_per_batch)
    return pl.pallas_call(
        _sgu_kernel,
        grid=(m // tm,),
        in_specs=[
            pl.BlockSpec((tm, width), lambda i: (i, su_block)),
            pl.BlockSpec((tm, width), lambda i: (i, su_block + 1)),
            pl.BlockSpec(w_s.shape, lambda i: (0, 0, 0)),
            pl.BlockSpec(b_s_lanes.shape, lambda i: (0, 0, 0)),
            pl.BlockSpec((1, width), lambda i: (0, 0)),
        ],
        out_specs=pl.BlockSpec((tm, width), lambda i: (i, 0)),
        out_shape=jax.ShapeDtypeStruct((m, width), BF16),
        compiler_params=_cparams(1),
        name="spatial_gating",
    )(p, p, w_s, b_s_lanes, g_v.reshape(1, width))


def _dft_cos_sin(n, scale):
    j = jnp.arange(n, dtype=jnp.int32)
    ang = ((j[:, None] * j[None, :]) % n).astype(F32) * (2.0 * math.pi / n)
    return jnp.cos(ang) * scale, jnp.sin(ang) * scale


def _fourier_stage1_kernel(x_ref, w_ref, tc_ref, ts_ref, o_ref):
    n2 = x_ref.shape[0]
    y = _dot(w_ref[...], x_ref[...])
    br, bi = y[:n2], y[n2:]
    reps = FOURIER_CH // LANES

    def widen(t):
        parts = [t[:, q * LANES:(q + 1) * LANES] for q in range(t.shape[1] // LANES)]
        return jnp.concatenate([p for p in parts for _ in range(reps)], axis=1)

    tc, ts = widen(tc_ref[...]), widen(ts_ref[...])
    o_ref[0] = (br * tc + bi * ts).astype(o_ref.dtype)
    o_ref[1] = (bi * tc - br * ts).astype(o_ref.dtype)


def _fourier_stage2_kernel(b_ref, w3_ref, cc_ref, sc_ref, o_ref):
    _, r, n1, ch = b_ref.shape
    w3, cc, sc = w3_ref[...], cc_ref[...], sc_ref[...]
    for j in range(r):
        bj = jnp.concatenate([b_ref[0, j], b_ref[1, j]], axis=0)
        y = _dot(w3, bj)
        out = _dot(y[:n1].astype(BF16), cc) + _dot(y[n1:].astype(BF16), sc)
        o_ref[:, j, :] = out


def _fourier_long(pf, *, batch, seq):
    n1 = LANES
    n2 = seq // n1
    ch = FOURIER_CH
    g_n = FOURIER_GROUPS
    r1 = min(16, n1)
    r2 = F32_SUBLANES

    c2, s2 = _dft_cos_sin(n2, n2 ** -0.5)
    w1 = jnp.concatenate([c2, -s2], axis=0).astype(BF16)

    kk = jnp.arange(n2, dtype=jnp.int32)[:, None] * jnp.arange(n1, dtype=jnp.int32)[None, :]
    ang = (kk % seq).astype(F32) * (2.0 * math.pi / seq)
    tw_c = jnp.broadcast_to(jnp.cos(ang)[:, :, None], (n2, n1, LANES)).reshape(n2, n1 * LANES)
    tw_s = jnp.broadcast_to(jnp.sin(ang)[:, :, None], (n2, n1, LANES)).reshape(n2, n1 * LANES)

    c1, s1 = _dft_cos_sin(n1, n1 ** -0.5)
    k3 = jnp.concatenate([jnp.concatenate([c1, s1], axis=1),
                          jnp.concatenate([-s1, c1], axis=1)], axis=0).astype(BF16)
    cc, sc = _dft_cos_sin(ch, ch ** -0.5)
    cc, sc = cc.astype(BF16), sc.astype(BF16)

    x2 = pf.reshape(batch, g_n, n2, n1 * ch)
    stage1 = pl.pallas_call(
        _fourier_stage1_kernel,
        grid=(batch, g_n, n1 // r1),
        in_specs=[
            pl.BlockSpec((None, None, n2, r1 * ch), lambda b, g, i: (b, g, 0, i)),
            _resident(w1.shape, lambda b, g, i: (0, 0)),
            pl.BlockSpec((n2, r1 * LANES), lambda b, g, i: (0, i)),
            pl.BlockSpec((n2, r1 * LANES), lambda b, g, i: (0, i)),
        ],
        out_specs=pl.BlockSpec((None, None, 2, n2, r1 * ch), lambda b, g, i: (b, g, 0, 0, i)),
        out_shape=jax.ShapeDtypeStruct((batch, g_n, 2, n2, n1 * ch), BF16),
        compiler_params=_cparams(3),
        name="fourier_stage1",
    )(x2, w1, tw_c, tw_s)
    stage1 = stage1.reshape(batch, g_n, 2, n2, n1, ch)

    out = pl.pallas_call(
        _fourier_stage2_kernel,
        grid=(batch, g_n, n2 // r2),
        in_specs=[
            pl.BlockSpec((None, None, 2, r2, n1, ch), lambda b, g, i: (b, g, 0, i, 0, 0)),
            _resident(k3.shape, lambda b, g, i: (0, 0)),
            _resident(cc.shape, lambda b, g, i: (0, 0)),
            _resident(sc.shape, lambda b, g, i: (0, 0)),
        ],
        out_specs=pl.BlockSpec((None, n1, r2, ch), lambda b, g, i: (b, 0, i, g)),
        out_shape=jax.ShapeDtypeStruct((batch, n1, n2, g_n * ch), F32),
        compiler_params=_cparams(3),
        name="fourier_stage2",
    )(stage1, k3, cc, sc)
    return out.reshape(batch * seq, g_n * ch)


def _fourier_dense_kernel(x_ref, cl_ref, sl_ref, cc_ref, sc_ref, o_ref):
    x = x_ref[...]
    a = _dot(cl_ref[...], x).astype(BF16)
    b = _dot(sl_ref[...], x).astype(BF16)
    o_ref[...] = _dot(a, cc_ref[...]) - _dot(b, sc_ref[...])


def _fourier_short(pf, *, batch, seq):
    ch = FOURIER_CH
    g_n = FOURIER_GROUPS
    cl, sl = _dft_cos_sin(seq, seq ** -0.5)
    cc, sc = _dft_cos_sin(ch, ch ** -0.5)
    const = lambda a: _resident(a.shape, lambda b, g: (0, 0))
    mats = [m.astype(BF16) for m in (cl, sl, cc, sc)]
    return pl.pallas_call(
        _fourier_dense_kernel,
        grid=(batch, g_n),
        in_specs=[pl.BlockSpec((None, None, seq, ch), lambda b, g: (b, g, 0, 0))] + [const(m) for m in mats],
        out_specs=pl.BlockSpec((seq, ch), lambda b, g: (b, g)),
        out_shape=jax.ShapeDtypeStruct((batch * seq, g_n * ch), F32),
        compiler_params=_cparams(2),
        name="fourier_dense",
    )(pf, *mats)


def _merge_kernel(att_ref, sgu_ref, four_ref, ga_ref, gg_ref, gf_ref, wa_ref, ws_ref, wf_ref, wo_ref,
                  x_ref, gate_ref, g_ref, sh_ref, sc_ref, xo_ref, ho_ref):
    y = jax.nn.sigmoid(ga_ref[...].astype(F32)) * _dot(att_ref[...], wa_ref[...])
    y = y + jax.nn.sigmoid(gg_ref[...].astype(F32)) * _dot(sgu_ref[...], ws_ref[...])
    y = y + jax.nn.sigmoid(gf_ref[...].astype(F32)) * _dot(four_ref[...].astype(BF16), wf_ref[...])
    x = x_ref[...] + gate_ref[...] * _dot(y.astype(BF16), wo_ref[...])
    xo_ref[...] = x
    h = x * lax.rsqrt(jnp.mean(x * x, axis=-1, keepdims=True) + EPS) * g_ref[...]
    ho_ref[...] = (h * (1.0 + sc_ref[...]) + sh_ref[...]).astype(ho_ref.dtype)


def _merge_project_norm(att, sgu, four, p, w_pa, w_ps, w_pf, w_o, x2d, gate, g, shift, scale, *,
                        gate_block, rows_per_batch):
    m, k = att.shape
    d = w_pa.shape[1]
    tm = min(256, rows_per_batch)
    tpb = rows_per_batch // tm
    act = pl.BlockSpec((tm, k), lambda i: (i, 0))
    gcol = lambda off: pl.BlockSpec((tm, d), lambda i: (i, gate_block + off))
    row = pl.BlockSpec((tm, d), lambda i: (i, 0))
    wspec = _resident((k, d), lambda i: (0, 0))
    vec = pl.BlockSpec((None, 1, d), lambda i: (i // tpb, 0, 0))
    return pl.pallas_call(
        _merge_kernel,
        grid=(m // tm,),
        in_specs=[act, act, act, gcol(0), gcol(1), gcol(2), wspec, wspec, wspec,
                  _resident((d, d), lambda i: (0, 0)), row, vec,
                  pl.BlockSpec((1, d), lambda i: (0, 0)), vec, vec],
        out_specs=[row, row],
        out_shape=[jax.ShapeDtypeStruct((m, d), F32), jax.ShapeDtypeStruct((m, d), BF16)],
        compiler_params=_cparams(1),
        name="merge_outproj_norm",
    )(att, sgu, four, p, p, p, w_pa, w_ps, w_pf, w_o, x2d, gate, g.reshape(1, d), shift, scale)


def _rope_tables(n_tokens):
    n_rows = n_tokens // GRID_W
    row = jnp.broadcast_to(jnp.arange(n_rows, dtype=F32)[:, None], (n_rows, GRID_W)).reshape(-1)
    col = jnp.broadcast_to(jnp.arange(GRID_W, dtype=F32)[None, :], (n_rows, GRID_W)).reshape(-1)
    n_freq = ATT_HD // 4
    inv = ROPE_BASE ** (-jnp.arange(n_freq, dtype=F32) / n_freq)
    ar = row[:, None] * inv
    ac = col[:, None] * inv
    ang = jnp.concatenate([ar, ar, ac, ac] * (LANES // ATT_HD), axis=-1)
    cos, sin = jnp.cos(ang), jnp.sin(ang)
    first_half = (jnp.arange(LANES) % (ATT_HD // 2)) < (ATT_HD // 4)
    return cos, jnp.where(first_half, -sin, 0.0), jnp.where(first_half, 0.0, sin)


def _split_in_projection(w_in):
    d = w_in.shape[0]
    qk = ATT_HEADS * ATT_HD
    v0 = 4 * qk
    su0 = v0 + ATT_HEADS * ATT_VD
    f0 = su0 + 2 * SGU_GROUPS * SGU_CH
    g0 = f0 + FOURIER_GROUPS * FOURIER_CH
    seg = lambda s: w_in[:, s * qk:(s + 1) * qk].astype(BF16).reshape(d, ATT_HEADS, ATT_HD)
    pair = lambda a, b: jnp.concatenate([seg(a), seg(b)], axis=-1).reshape(d, 2 * qk)
    w_qk = jnp.concatenate([pair(0, 1), pair(2, 3)], axis=-1)
    w_rest = jnp.concatenate([w_in[:, g0:].astype(BF16), w_in[:, v0:f0].astype(BF16)], axis=-1)
    return w_qk, w_rest, w_in[:, f0:g0].astype(BF16)


def kernel(x, c, ctx, c_ctx, w_mod, b_mod, norm1_g, norm2_g, w_in, lambda_q1, lambda_k1, lambda_q2, lambda_k2,
           subln_g, sgu_norm_g, sgu_w, sgu_b, w_proj_att, w_proj_sgu, w_proj_fourier, w_out, w_mlp_in, w_mlp_out,
           final_g):
    batch, seq, d = x.shape
    ctx_len = ctx.shape[1]
    depth = w_mod.shape[0]
    in_width = w_in.shape[2]
    qk_width = 2 * ATT_HEADS * ATT_HD
    v_width = ATT_HEADS * ATT_VD
    sgu_width = SGU_GROUPS * SGU_CH
    four_width = FOURIER_GROUPS * FOURIER_CH
    assert in_width == 2 * qk_width + v_width + 2 * sgu_width + four_width + 3 * d
    assert seq % (LANES * F32_SUBLANES) == 0 and seq % GRID_W == 0 and ctx_len % 256 == 0
    gate_block = 0
    v_block = 3 * d // LANES
    su_block = (3 * d + v_width) // sgu_width
    assert (3 * d + v_width) % sgu_width == 0

    sgu_b_lanes = jnp.broadcast_to(sgu_b[..., None], sgu_b.shape + (SGU_CH,))

    cos, sin_up, sin_down = _rope_tables(seq)
    ctx_rows = min(1024, ctx_len)
    one_t, zero_t = jnp.ones((ctx_rows, LANES), F32), jnp.zeros((ctx_rows, LANES), F32)

    pad = jnp.zeros((F32_SUBLANES - batch - 1, d), F32)
    c_rows = jnp.concatenate([c, c_ctx[None, :], pad], axis=0)
    mod = _modulation(c_rows, w_mod, b_mod)

    xs = x.reshape(batch * seq, d)
    cs = ctx.reshape(batch * ctx_len, d)
    for l in range(depth):
        last = l == depth - 1
        lambda_init = 0.8 - 0.6 * math.exp(-0.3 * l)
        lam_rows = jnp.stack([lambda_q1[l], lambda_k1[l], lambda_q2[l], lambda_k2[l]]).astype(F32)
        lam_rows = jnp.pad(lam_rows, ((0, F32_SUBLANES - 4), (0, LANES - ATT_HD)))
        mod_x = mod[l, :batch].reshape(batch, 1, N_MOD, d)
        mod_c = mod[l, batch:batch + 1].reshape(1, 1, N_MOD, d)
        mx = [mod_x[:, :, i] for i in range(N_MOD)]
        mc = [mod_c[:, :, i] for i in range(N_MOD)]

        w_qk, w_rest, w_four = _split_in_projection(w_in[l])
        w_pa, w_ps, w_pf = (w[l].astype(BF16) for w in (w_proj_att, w_proj_sgu, w_proj_fourier))
        w_o, w_1, w_2 = (w[l].astype(BF16) for w in (w_out, w_mlp_in, w_mlp_out))
        sgu_w_b = sgu_w[l].astype(BF16)

        hx = _rmsnorm(xs, norm1_g[l], mx[0], mx[1], rows_per_batch=seq, out_dtype=BF16)
        hc = _rmsnorm(cs, norm1_g[l], mc[0], mc[1], rows_per_batch=batch * ctx_len, out_dtype=BF16)
        qk_x = _in_projection_qk(hx, w_qk, cos, sin_up, sin_down, rows_per_batch=seq)
        qk_c = _in_projection_qk(hc, w_qk, one_t, zero_t, zero_t, rows_per_batch=ctx_len)
        px = _in_projection_plain(hx, w_rest)
        pc = _in_projection_plain(hc, w_rest)
        fx = _in_projection_fourier(hx, w_four, batch=batch)

        att_x = _diff_attention(lam_rows, subln_g[l], qk_x, [(qk_x, px), (qk_c, pc)], batch=batch, q_len=seq,
                                key_lens=[seq, ctx_len], lambda_init=lambda_init, v_block=v_block)
        sgu_x = _spatial_gating(px, sgu_w_b, sgu_b_lanes[l], sgu_norm_g[l], su_block=su_block,
                                rows_per_batch=seq)
        four_x = _fourier_long(fx, batch=batch, seq=seq)

        if not last:
            fc = _in_projection_fourier(hc, w_four, batch=batch)
            att_c = _diff_attention(lam_rows, subln_g[l], qk_c, [(qk_c, pc)], batch=batch, q_len=ctx_len,
                                    key_lens=[ctx_len], lambda_init=lambda_init, v_block=v_block)
            sgu_c = _spatial_gating(pc, sgu_w_b, sgu_b_lanes[l], sgu_norm_g[l], su_block=su_block,
                                    rows_per_batch=ctx_len)
            four_c = _fourier_short(fc, batch=batch, seq=ctx_len)
            cs, hc2 = _merge_project_norm(att_c, sgu_c, four_c, pc, w_pa, w_ps, w_pf, w_o, cs,
                                          mc[2], norm2_g[l], mc[3], mc[4], gate_block=gate_block,
                                          rows_per_batch=batch * ctx_len)
            cs = _mlp_gated_residual(hc2, w_1, w_2, cs, mc[5], rows_per_batch=batch * ctx_len)

        xs, hx2 = _merge_project_norm(att_x, sgu_x, four_x, px, w_pa, w_ps, w_pf, w_o, xs,
                                      mx[2], norm2_g[l], mx[3], mx[4], gate_block=gate_block,
                                      rows_per_batch=seq)
        xs = _mlp_gated_residual(hx2, w_1, w_2, xs, mx[5], rows_per_batch=seq)

    out = _rmsnorm(xs, final_g, rows_per_batch=seq, out_dtype=x.dtype)
    return out.reshape(batch, seq, d)
```
